```python
import math, functools
import jax, jax.numpy as jnp
from jax import lax
import numpy as np

D_MODEL = 4096
BATCH = 4
SEQ = 2048
DEPTH = 1
DEC_BATCH = 32
DEC_SEQ = 1
PAST_LEN = 8192
PAGE_SIZE = 128

MIX_WIDTH = D_MODEL
ATT_WIDTH = MIX_WIDTH // 2
RNN_WIDTH = MIX_WIDTH - ATT_WIDTH
HEAD_DIM = 128
H_ATT = ATT_WIDTH // (2 * HEAD_DIM)
ATT_QK = H_ATT * 2 * HEAD_DIM
RNN_DK = 128
RNN_DV = 128
H_RNN = RNN_WIDTH // RNN_DV
RNN_K = H_RNN * RNN_DK
IN_COLS = 3 * ATT_QK + 2 * RNN_K + 2 * RNN_WIDTH
Q_BLOCK = 128
HGRN_CHUNK = 64
ATTN_SCALE = HEAD_DIM ** -0.5
NORM_EPS = 1e-6
N_KEYS = 128
N_EXPERTS = N_KEYS ** 2
PEER_HEADS = 8
PEER_TOPK = 16
PEER_DKEY = 256
PEER_BLOCK = 128

kernel_name = "hymba_diffattn_hgrn2_peer_step"


def rms_norm(x, w):
    xf = x.astype(jnp.float32)
    y = xf * lax.rsqrt(jnp.mean(xf * xf, axis=-1, keepdims=True) + NORM_EPS)
    return (y * w.astype(jnp.float32)).astype(x.dtype)


def lambda_init(layer_idx):
    return 0.8 - 0.6 * math.exp(-0.3 * layer_idx)


def project(x, norm1_w, w_in, q_norm_w, k_norm_w):
    B, T, _ = x.shape
    z = rms_norm(x, norm1_w) @ w_in
    sizes = (ATT_QK, ATT_QK, ATT_QK, RNN_K, RNN_K, RNN_WIDTH, RNN_WIDTH)
    qa, ka, va, qr, fr, ir, gr = jnp.split(z, np.cumsum(sizes)[:-1].tolist(), axis=-1)
    qa = rms_norm(qa.reshape(B, T, H_ATT, 2, HEAD_DIM), q_norm_w)
    ka = rms_norm(ka.reshape(B, T, H_ATT, 2, HEAD_DIM), k_norm_w)
    va = va.reshape(B, T, H_ATT, 2 * HEAD_DIM)
    return qa, ka, va, qr, fr, ir, gr


def diff_probs(s, lam):
    p = jax.nn.softmax(s, axis=-1)
    return p[:, :, 0] - lam * p[:, :, 1]


def diff_attn_prompt(q, k, v, lam):
    B, S = q.shape[:2]
    qb = min(Q_BLOCK, S)
    nb = S // qb
    q_blocks = q.reshape(B, nb, qb, H_ATT, 2, HEAD_DIM).transpose(1, 0, 2, 3, 4, 5)
    k_pos = jnp.arange(S)

    def one_block(args):
        q_blk, blk_id = args
        q_pos = blk_id * qb + jnp.arange(qb)
        s = jnp.einsum('bqhcd,bkhcd->bhcqk', q_blk, k).astype(jnp.float32) * ATTN_SCALE
        s = jnp.where(k_pos[None, :] <= q_pos[:, None], s, -jnp.inf)
        a = diff_probs(s, lam).astype(v.dtype)
        return jnp.einsum('bhqk,bkhe->bqhe', a, v)

    o = lax.map(one_block, (q_blocks, jnp.arange(nb)))
    return o.transpose(1, 0, 2, 3, 4).reshape(B, S, H_ATT, 2 * HEAD_DIM)


def diff_attn_sample(q, k, v, lam, k_past, v_past):
    T = q.shape[1]
    P = k_past.shape[1]
    s_past = jnp.einsum('bqhcd,bkhcd->bhcqk', q, k_past).astype(jnp.float32) * ATTN_SCALE
    s_new = jnp.einsum('bqhcd,bkhcd->bhcqk', q, k).astype(jnp.float32) * ATTN_SCALE
    causal = jnp.tril(jnp.ones((T, T), dtype=bool))
    s_new = jnp.where(causal, s_new, -jnp.inf)
    a = diff_probs(jnp.concatenate([s_past, s_new], axis=-1), lam).astype(v.dtype)
    return (jnp.einsum('bhqk,bkhe->bqhe', a[..., :P], v_past)
            + jnp.einsum('bhqk,bkhe->bqhe', a[..., P:], v))


def hgrn2_chunked(q, k, v, logf, s0):
    B, T, H, DK = q.shape
    DV = v.shape[-1]
    C = min(HGRN_CHUNK, T)
    pad = (-T) % C
    if pad:
        pw = ((0, 0), (0, pad), (0, 0), (0, 0))
        q, k, v, logf = (jnp.pad(a, pw) for a in (q, k, v, logf))
    n = (T + pad) // C
    to_chunks = lambda a: a.reshape(B, n, C, H, a.shape[-1]).transpose(1, 0, 3, 2, 4)
    qc, kc, vc, gc = (to_chunks(a) for a in (q, k, v, logf))
    mask = jnp.tril(jnp.ones((C, C), dtype=bool))[:, :, None]

    def step(S, inp):
        qt, kt, vt, gt = inp
        b = jnp.cumsum(gt, axis=2)
        inter = jnp.einsum('bhtk,bhkv->bhtv', qt * jnp.exp(b), S)
        dlt = b[:, :, :, None, :] - b[:, :, None, :, :]
        decay = jnp.where(mask, jnp.exp(jnp.where(mask, dlt, 0.0)), 0.0)
        att = jnp.sum(qt[:, :, :, None, :] * kt[:, :, None, :, :] * decay, axis=-1)
        intra = jnp.einsum('bhts,bhsv->bhtv', att, vt)
        b_last = b[:, :, -1:, :]
        S_new = (jnp.exp(b_last[:, :, 0, :])[..., None] * S
                 + jnp.einsum('bhsk,bhsv->bhkv', kt * jnp.exp(b_last - b), vt))
        return S_new, inter + intra

    s_fin, o = lax.scan(step, s0, (qc, kc, vc, gc))
    o = o.transpose(1, 0, 3, 2, 4).reshape(B, n * C, H, DV)[:, :T]
    return o, s_fin


def hgrn2_mixer(qr, fr, ir, gr, lb, s0, rnn_norm_w):
    B, T = qr.shape[:2]
    f32 = jnp.float32
    zf = fr.reshape(B, T, H_RNN, RNN_DK).astype(f32)
    lbh = lb.reshape(H_RNN, RNN_DK)
    logf = jnp.logaddexp(jnp.log(lbh), jnp.log1p(-lbh) + jax.nn.log_sigmoid(zf))
    k = (1.0 - lbh) * jax.nn.sigmoid(-zf)
    q = jax.nn.silu(qr.reshape(B, T, H_RNN, RNN_DK).astype(f32))
    v = ir.reshape(B, T, H_RNN, RNN_DV).astype(f32)
    o, s_fin = hgrn2_chunked(q, k, v, logf, s0.astype(f32))
    o = rms_norm(o, rnn_norm_w) * jax.nn.silu(gr.reshape(B, T, H_RNN, RNN_DV).astype(f32))
    return o.reshape(B, T, RNN_WIDTH).astype(qr.dtype), s_fin


def peer_ffn(x, wq, subkeys, u_tab, v_tab):
    B, T, D = x.shape
    M = B * T
    xf = x.reshape(M, D)
    q = (xf @ wq).reshape(M, PEER_HEADS, 2, PEER_DKEY // 2)
    sc = jnp.einsum('mhcd,hcnd->mhcn', q, subkeys).astype(jnp.float32)
    s_half, i_half = lax.top_k(sc, PEER_TOPK)
    cand = s_half[:, :, 0, :, None] + s_half[:, :, 1, None, :]
    cand_idx = i_half[:, :, 0, :, None] * N_KEYS + i_half[:, :, 1, None, :]
    best, pos = lax.top_k(cand.reshape(M, PEER_HEADS, PEER_TOPK * PEER_TOPK), PEER_TOPK)
    e_idx = jnp.take_along_axis(cand_idx.reshape(M, PEER_HEADS, -1), pos, axis=-1)
    gate = jax.nn.softmax(best, axis=-1).astype(x.dtype)
    blk = min(PEER_BLOCK, M)
    pad = (-M) % blk
    if pad:
        xf = jnp.pad(xf, ((0, pad), (0, 0)))
        e_idx = jnp.pad(e_idx, ((0, pad), (0, 0), (0, 0)))
        gate = jnp.pad(gate, ((0, pad), (0, 0), (0, 0)))
    nb = (M + pad) // blk

    def expert_block(args):
        xb, eb, gb = args
        hid = jax.nn.gelu(jnp.einsum('md,mhkd->mhk', xb, u_tab[eb]), approximate=False)
        return jnp.einsum('mhk,mhkd->md', gb * hid, v_tab[eb])

    out = lax.map(expert_block, (xf.reshape(nb, blk, D),
                                 e_idx.reshape(nb, blk, PEER_HEADS, PEER_TOPK),
                                 gate.reshape(nb, blk, PEER_HEADS, PEER_TOPK)))
    return out.reshape(nb * blk, D)[:M].reshape(B, T, D)


def trunk_layer(x, s0, attend, lam, lam_init, norm1_w, w_in, q_norm_w, k_norm_w, sub_norm_w,
                lb, rnn_norm_w, w_out, norm2_w, peer_wq, peer_subkeys, peer_u, peer_v):
    B, T, _ = x.shape
    qa, ka, va, qr, fr, ir, gr = project(x, norm1_w, w_in, q_norm_w, k_norm_w)
    att = attend(qa, ka, va, lam)
    att = (rms_norm(att, sub_norm_w) * (1.0 - lam_init)).reshape(B, T, ATT_WIDTH).astype(x.dtype)
    rnn, s_fin = hgrn2_mixer(qr, fr, ir, gr, lb, s0, rnn_norm_w)
    h = x + jnp.concatenate([att, rnn], axis=-1) @ w_out
    y = h + peer_ffn(rms_norm(h, norm2_w), peer_wq, peer_subkeys, peer_u, peer_v).astype(h.dtype)
    return y, ka, va, s_fin


def setup_inputs(seed: int = 0) -> dict:
    key = jax.random.key(seed)
    ks = jax.random.split(key, 24)
    n_pages = PAST_LEN // PAGE_SIZE
    n_used = DEC_BATCH * n_pages
    n_phys = n_used + max(1, n_used // 4)
    nrm = lambda k, shape, s: jax.random.normal(k, shape, jnp.float32) * s
    gain = lambda k, shape: 1.0 + nrm(k, shape, 0.05)
    page_table = jax.random.permutation(ks[5], n_phys)[:n_used].reshape(DEC_BATCH, n_pages).astype(jnp.int32)
    return {
        "x_prompt": nrm(ks[0], (BATCH, SEQ, D_MODEL), 1.0),
        "x_sample": nrm(ks[1], (DEC_BATCH, DEC_SEQ, D_MODEL), 1.0),
        "cache_k": nrm(ks[2], (DEPTH, n_phys, PAGE_SIZE, H_ATT, 2, HEAD_DIM), 1.0),
        "cache_v": nrm(ks[3], (DEPTH, n_phys, PAGE_SIZE, H_ATT, 2 * HEAD_DIM), 1.0),
        "state_rnn": nrm(ks[4], (DEPTH, DEC_BATCH, H_RNN, RNN_DK, RNN_DV), 0.5),
        "page_table": page_table,
        "norm1_w": gain(ks[6], (DEPTH, D_MODEL)),
        "w_in": nrm(ks[7], (DEPTH, D_MODEL, IN_COLS), D_MODEL ** -0.5),
        "q_norm_w": gain(ks[8], (DEPTH, HEAD_DIM)),
        "k_norm_w": gain(ks[9], (DEPTH, HEAD_DIM)),
        "lambda_q1": nrm(ks[10], (DEPTH, HEAD_DIM), 0.1),
        "lambda_k1": nrm(ks[11], (DEPTH, HEAD_DIM), 0.1),
        "lambda_q2": nrm(ks[12], (DEPTH, HEAD_DIM), 0.1),
        "lambda_k2": nrm(ks[13], (DEPTH, HEAD_DIM), 0.1),
        "sub_norm_w": gain(ks[14], (DEPTH, 2 * HEAD_DIM)),
        "lb_param": nrm(ks[15], (DEPTH + 1, RNN_K), 0.1),
        "rnn_norm_w": gain(ks[16], (DEPTH, RNN_DV)),
        "w_out": nrm(ks[17], (DEPTH, MIX_WIDTH, D_MODEL), MIX_WIDTH ** -0.5),
        "norm2_w": gain(ks[18], (DEPTH, D_MODEL)),
        "peer_wq": nrm(ks[19], (DEPTH, D_MODEL, PEER_HEADS * PEER_DKEY), D_MODEL ** -0.5),
        "peer_subkeys": nrm(ks[20], (DEPTH, PEER_HEADS, 2, N_KEYS, PEER_DKEY // 2), (PEER_DKEY // 2) ** -0.5),
        "peer_u": nrm(ks[21], (DEPTH, N_EXPERTS, D_MODEL), D_MODEL ** -0.5),
        "peer_v": nrm(ks[22], (DEPTH, N_EXPERTS, D_MODEL), PEER_HEADS ** -0.5),
    }


def reference(x_prompt, x_sample, cache_k, cache_v, state_rnn, page_table, norm1_w, w_in,
              q_norm_w, k_norm_w, lambda_q1, lambda_k1, lambda_q2, lambda_k2, sub_norm_w,
              lb_param, rnn_norm_w, w_out, norm2_w, peer_wq, peer_subkeys, peer_u, peer_v):
    f32 = jnp.float32
    lb_all = jnp.cumsum(jax.nn.softmax(lb_param.astype(f32), axis=0), axis=0)
    n_dec = x_sample.shape[0]
    y_p, y_s = x_prompt, x_sample
    kp_l, vp_l, ks_l, vs_l, sp_l, ss_l = [], [], [], [], [], []
    for l in range(DEPTH):
        lam_init = lambda_init(l)
        lam = (jnp.exp(jnp.sum(lambda_q1[l].astype(f32) * lambda_k1[l].astype(f32)))
               - jnp.exp(jnp.sum(lambda_q2[l].astype(f32) * lambda_k2[l].astype(f32))) + lam_init)
        weights = (norm1_w[l], w_in[l], q_norm_w[l], k_norm_w[l], sub_norm_w[l], lb_all[l],
                   rnn_norm_w[l], w_out[l], norm2_w[l], peer_wq[l], peer_subkeys[l], peer_u[l], peer_v[l])
        s0 = jnp.zeros((y_p.shape[0], H_RNN, RNN_DK, RNN_DV), f32)
        y_p, kp, vp, sp = trunk_layer(y_p, s0, diff_attn_prompt, lam, lam_init, *weights)
        k_past = cache_k[l][page_table].reshape(n_dec, -1, H_ATT, 2, HEAD_DIM)
        v_past = cache_v[l][page_table].reshape(n_dec, -1, H_ATT, 2 * HEAD_DIM)
        attend_s = functools.partial(diff_attn_sample, k_past=k_past, v_past=v_past)
        y_s, ks_, vs_, ss = trunk_layer(y_s, state_rnn[l], attend_s, lam, lam_init, *weights)
        kp_l.append(kp); vp_l.append(vp); sp_l.append(sp.astype(state_rnn.dtype))
        ks_l.append(ks_); vs_l.append(vs_); ss_l.append(ss.astype(state_rnn.dtype))
    return (y_p, y_s, jnp.stack(kp_l), jnp.stack(vp_l), jnp.stack(ks_l), jnp.stack(vs_l),
            jnp.stack(sp_l), jnp.stack(ss_l))
```

```python
import functools
import math

import jax
import jax.numpy as jnp
from jax import lax
from jax.experimental import pallas as pl
from jax.experimental.pallas import tpu as pltpu

F32 = jnp.float32
BF16 = jnp.bfloat16

LANES = 128
LOG2_LANES = 7
HEAD_DIM = 128
VAL_DIM = 2 * HEAD_DIM
RNN_DK = 128
RNN_DV = 128
HGRN_CHUNK = 16
NORM_EPS = 1e-6
ATTN_SCALE = HEAD_DIM ** -0.5
N_KEYS = 128
PEER_TOPK = 16
PAGE_SIZE = 128
GATE_ROWS = 8
VMEM_LIMIT = 56 * 1024 * 1024

NT_DIMS = (((1,), (1,)), ((), ()))
TN_DIMS = (((0,), (0,)), ((), ()))


def _params(*sem):
    return pltpu.CompilerParams(dimension_semantics=sem, vmem_limit_bytes=VMEM_LIMIT)


def _sigmoid(x):
    return 1.0 / (1.0 + jnp.exp(-x))


def _rmsnorm_kernel(x_ref, w_ref, o_ref):
    x = x_ref[...]
    ms = jnp.mean(x * x, axis=-1, keepdims=True)
    o_ref[...] = (x * lax.rsqrt(ms + NORM_EPS) * w_ref[...]).astype(o_ref.dtype)


def rmsnorm_bf16(x, w, tm):
    m, d = x.shape
    return pl.pallas_call(
        _rmsnorm_kernel,
        grid=(m // tm,),
        in_specs=[pl.BlockSpec((tm, d), lambda i: (i, 0)),
                  pl.BlockSpec((1, d), lambda i: (0, 0))],
        out_specs=pl.BlockSpec((tm, d), lambda i: (i, 0)),
        out_shape=jax.ShapeDtypeStruct((m, d), BF16),
        compiler_params=_params("parallel"),
        name="rmsnorm",
    )(x, w.reshape(1, d))


def _matmul_kernel(*refs, has_res):
    if has_res:
        x_ref, w_ref, r_ref, o_ref, wb_ref = refs
    else:
        x_ref, w_ref, o_ref, wb_ref = refs

    @pl.when(pl.program_id(1) == 0)
    def _():
        wb_ref[...] = w_ref[...].astype(BF16)

    acc = jnp.dot(x_ref[...], wb_ref[...], preferred_element_type=F32)
    if has_res:
        acc = r_ref[...] + acc
    o_ref[...] = acc


def matmul(x, w, res=None, *, tm, tn):
    m, k = x.shape
    n = w.shape[1]
    in_specs = [pl.BlockSpec((tm, k), lambda j, i: (i, 0)),
                pl.BlockSpec((k, tn), lambda j, i: (0, j))]
    args = [x, w]
    if res is not None:
        in_specs.append(pl.BlockSpec((tm, tn), lambda j, i: (i, j)))
        args.append(res)
    return pl.pallas_call(
        functools.partial(_matmul_kernel, has_res=res is not None),
        grid=(n // tn, m // tm),
        in_specs=in_specs,
        out_specs=pl.BlockSpec((tm, tn), lambda j, i: (i, j)),
        out_shape=jax.ShapeDtypeStruct((m, n), F32),
        scratch_shapes=[pltpu.VMEM((k, tn), BF16)],
        compiler_params=_params("arbitrary", "arbitrary"),
        name="matmul",
    )(*args)


def _qkv_kernel(zq_ref, zk_ref, zv_ref, qw_ref, kw_ref, qn_ref, kn_ref, knb_ref, v_ref, vb_ref):
    width = zq_ref.shape[1]
    for g in range(width // HEAD_DIM):
        sl = slice(g * HEAD_DIM, (g + 1) * HEAD_DIM)
        xq = zq_ref[:, sl]
        yq = xq * lax.rsqrt(jnp.mean(xq * xq, axis=-1, keepdims=True) + NORM_EPS) * qw_ref[...]
        qn_ref[:, sl] = yq.astype(BF16)
        xk = zk_ref[:, sl]
        yk = xk * lax.rsqrt(jnp.mean(xk * xk, axis=-1, keepdims=True) + NORM_EPS) * kw_ref[...]
        kn_ref[:, sl] = yk
        knb_ref[:, sl] = yk.astype(BF16)
    v = zv_ref[...]
    v_ref[...] = v
    vb_ref[...] = v.astype(BF16)


def qkv_post(z, q_norm_w, k_norm_w, width, tm):
    m = z.shape[0]
    blk = lambda c: pl.BlockSpec((tm, width), lambda i, c=c: (i, c))
    wspec = pl.BlockSpec((1, HEAD_DIM), lambda i: (0, 0))
    out = lambda dt: jax.ShapeDtypeStruct((m, width), dt)
    return pl.pallas_call(
        _qkv_kernel,
        grid=(m // tm,),
        in_specs=[blk(0), blk(1), blk(2), wspec, wspec],
        out_specs=[blk(0)] * 5,
        out_shape=[out(BF16), out(F32), out(BF16), out(F32), out(BF16)],
        compiler_params=_params("parallel"),
        name="qkv_post",
    )(z, z, z, q_norm_w.reshape(1, HEAD_DIM), k_norm_w.reshape(1, HEAD_DIM))


def _lambda(lq1_ref, lk1_ref, lq2_ref, lk2_ref, lam_init):
    a = jnp.sum(lq1_ref[...] * lk1_ref[...], axis=-1, keepdims=True)
    b = jnp.sum(lq2_ref[...] * lk2_ref[...], axis=-1, keepdims=True)
    return jnp.exp(a) - jnp.exp(b) + lam_init


def _sub_norm(o, w, lam_init):
    ms = jnp.mean(o * o, axis=-1, keepdims=True)
    return o * lax.rsqrt(ms + NORM_EPS) * w * (1.0 - lam_init)


def _attn_prompt_kernel(q_ref, k_ref, v_ref, lq1_ref, lk1_ref, lq2_ref, lk2_ref, sw_ref, o_ref,
                        m_ref, l_ref, acc_ref, *, tq, lam_init):
    qi = pl.program_id(2)
    m_ref[...] = jnp.full(m_ref.shape, -jnp.inf, F32)
    l_ref[...] = jnp.zeros(l_ref.shape, F32)
    acc_ref[...] = jnp.zeros(acc_ref.shape, F32)
    row = qi * tq + lax.broadcasted_iota(jnp.int32, (tq, tq), 0)

    def body(j, carry):
        col = j * tq + lax.broadcasted_iota(jnp.int32, (tq, tq), 1)
        keep = col <= row
        vblk = v_ref[pl.ds(j * tq, tq), :]
        for c in range(2):
            sl = slice(c * HEAD_DIM, (c + 1) * HEAD_DIM)
            s = lax.dot_general(q_ref[:, sl], k_ref[pl.ds(j * tq, tq), sl], NT_DIMS,
                                preferred_element_type=F32) * ATTN_SCALE
            s = jnp.where(keep, s, -jnp.inf)
            m_old = m_ref[c]
            m_new = jnp.maximum(m_old, jnp.max(s, axis=-1, keepdims=True))
            alpha = jnp.exp(m_old - m_new)
            p = jnp.exp(s - m_new)
            l_ref[c] = alpha * l_ref[c] + jnp.sum(p, axis=-1, keepdims=True)
            acc_ref[c] = alpha * acc_ref[c] + jnp.dot(p.astype(BF16), vblk, preferred_element_type=F32)
            m_ref[c] = m_new
        return carry

    lax.fori_loop(0, qi + 1, body, 0)
    lam = _lambda(lq1_ref, lk1_ref, lq2_ref, lk2_ref, lam_init)
    o = acc_ref[0] / l_ref[0] - lam * (acc_ref[1] / l_ref[1])
    o_ref[...] = _sub_norm(o, sw_ref[...], lam_init).astype(o_ref.dtype)


def attn_prompt(qn, knb, vb, lams, sub_norm_w, batch, seq, lam_init, tq):
    m, width = qn.shape
    heads = width // VAL_DIM
    nq = seq // tq
    vec = pl.BlockSpec((1, HEAD_DIM), lambda b, h, i: (0, 0))
    return pl.pallas_call(
        functools.partial(_attn_prompt_kernel, tq=tq, lam_init=lam_init),
        grid=(batch, heads, nq),
        in_specs=[pl.BlockSpec((tq, VAL_DIM), lambda b, h, i: (b * nq + i, h)),
                  pl.BlockSpec((seq, VAL_DIM), lambda b, h, i: (b, h)),
                  pl.BlockSpec((seq, VAL_DIM), lambda b, h, i: (b, h)),
                  vec, vec, vec, vec,
                  pl.BlockSpec((1, VAL_DIM), lambda b, h, i: (0, 0))],
        out_specs=pl.BlockSpec((tq, VAL_DIM), lambda b, h, i: (b * nq + i, h)),
        out_shape=jax.ShapeDtypeStruct((m, width), BF16),
        scratch_shapes=[pltpu.VMEM((2, tq, 1), F32), pltpu.VMEM((2, tq, 1), F32),
                        pltpu.VMEM((2, tq, VAL_DIM), F32)],
        compiler_params=_params("parallel", "parallel", "arbitrary"),
        name="attn_prompt",
    )(qn, knb, vb, *lams, sub_norm_w.reshape(1, VAL_DIM))


def _attn_sample_kernel(pt_ref, q_ref, kc_ref, vc_ref, kn_ref, vn_ref, lq1_ref, lk1_ref, lq2_ref,
                        lk2_ref, sw_ref, o_ref, qbd_ref, m_ref, l_ref, acc_ref, *, lam_init):
    del pt_ref
    p_id = pl.program_id(1)
    nsub, width = qbd_ref.shape
    heads = nsub // 2

    @pl.when(p_id == 0)
    def _():
        r = lax.broadcasted_iota(jnp.int32, (nsub, width), 0)
        cgrp = jnp.right_shift(lax.broadcasted_iota(jnp.int32, (nsub, width), 1), LOG2_LANES)
        qb = jnp.broadcast_to(q_ref[...].astype(F32), (nsub, width))
        qbd_ref[...] = jnp.where(cgrp == r, qb, 0.0).astype(BF16)
        m_ref[...] = jnp.full(m_ref.shape, -jnp.inf, F32)
        l_ref[...] = jnp.zeros(l_ref.shape, F32)
        acc_ref[...] = jnp.zeros(acc_ref.shape, F32)

    def update(s, pv_fn):
        m_old = m_ref[...]
        m_new = jnp.maximum(m_old, jnp.max(s, axis=-1, keepdims=True))
        alpha = jnp.exp(m_old - m_new)
        p = jnp.exp(s - m_new)
        l_ref[...] = alpha * l_ref[...] + jnp.sum(p, axis=-1, keepdims=True)
        acc_ref[...] = alpha * acc_ref[...] + pv_fn(p)
        m_ref[...] = m_new

    kpage = kc_ref[...].astype(BF16)
    vpage = vc_ref[...].astype(BF16)
    s = lax.dot_general(qbd_ref[...], kpage, NT_DIMS, preferred_element_type=F32) * ATTN_SCALE
    update(s, lambda p: jnp.dot(p.astype(BF16), vpage, preferred_element_type=F32))

    @pl.when(p_id == pl.num_programs(1) - 1)
    def _():
        kn = kn_ref[...].astype(BF16).astype(F32)
        vn = vn_ref[...].astype(BF16).astype(F32)
        s_new = jnp.sum(qbd_ref[...].astype(F32) * kn, axis=-1, keepdims=True) * ATTN_SCALE
        update(s_new, lambda p: p.astype(BF16).astype(F32) * vn)
        lam = _lambda(lq1_ref, lk1_ref, lq2_ref, lk2_ref, lam_init)
        o_all = acc_ref[...] / l_ref[...]
        for h in range(heads):
            sl = slice(h * VAL_DIM, (h + 1) * VAL_DIM)
            o = o_all[2 * h:2 * h + 1, sl] - lam * o_all[2 * h + 1:2 * h + 2, sl]
            o_ref[:, sl] = _sub_norm(o, sw_ref[...], lam_init).astype(o_ref.dtype)


def attn_sample(qn, kn, v, cache_k, cache_v, page_table, lams, sub_norm_w, lam_init):
    bsz, _, width = qn.shape
    n_pages = page_table.shape[1]
    nsub = width // HEAD_DIM
    row = pl.BlockSpec((None, 1, width), lambda b, p, pt: (b, 0, 0))
    page = pl.BlockSpec((None, PAGE_SIZE, width), lambda b, p, pt: (pt[b * n_pages + p], 0, 0))
    vec = pl.BlockSpec((1, HEAD_DIM), lambda b, p, pt: (0, 0))
    grid_spec = pltpu.PrefetchScalarGridSpec(
        num_scalar_prefetch=1,
        grid=(bsz, n_pages),
        in_specs=[row, page, page, row, row, vec, vec, vec, vec,
                  pl.BlockSpec((1, VAL_DIM), lambda b, p, pt: (0, 0))],
        out_specs=row,
        scratch_shapes=[pltpu.VMEM((nsub, width), BF16), pltpu.VMEM((nsub, 1), F32),
                        pltpu.VMEM((nsub, 1), F32), pltpu.VMEM((nsub, width), F32)],
    )
    return pl.pallas_call(
        functools.partial(_attn_sample_kernel, lam_init=lam_init),
        grid_spec=grid_spec,
        out_shape=jax.ShapeDtypeStruct((bsz, 1, width), BF16),
        compiler_params=_params("parallel", "arbitrary"),
        name="attn_sample",
    )(page_table.reshape(-1), qn, cache_k, cache_v, kn, v, *lams, sub_norm_w.reshape(1, VAL_DIM))


def _lower_bound(lb_ref):
    p = lb_ref[...]
    e = jnp.exp(p - jnp.max(p, axis=0, keepdims=True))
    return e[0:1, :] / jnp.sum(e, axis=0, keepdims=True)


def _split3(x):
    hi = x.astype(BF16)
    r = x - hi.astype(F32)
    mid = r.astype(BF16)
    lo = (r - mid.astype(F32)).astype(BF16)
    return hi, mid, lo


def _hgrn_prompt_kernel(zq_ref, zf_ref, zi_ref, zg_ref, lb_ref, s0_ref, nw_ref, o_ref, sfin_ref,
                        q_s, k_s, b_s, oi_s, st_s, dec_s, *, seq):
    c = HGRN_CHUNK
    n_chunks = seq // c
    lb = _lower_bound(lb_ref)
    zf = zf_ref[...]
    f = lb + (1.0 - lb) * _sigmoid(zf)
    k_s[...] = (1.0 - lb) * _sigmoid(-zf)
    zq = zq_ref[...]
    q_s[...] = zq * _sigmoid(zq)
    logf = jnp.log(f)

    grp = 256
    r = lax.broadcasted_iota(jnp.int32, (grp, grp), 0)
    cc = lax.broadcasted_iota(jnp.int32, (grp, grp), 1)
    shift = c.bit_length() - 1
    same_chunk = jnp.right_shift(r, shift) == jnp.right_shift(cc, shift)
    tri = jnp.where(same_chunk & (cc <= r), 1.0, 0.0).astype(BF16)
    b_s[...] = logf
    for gi in range(seq // grp):
        rows = slice(gi * grp, (gi + 1) * grp)
        hi, mid, lo = _split3(b_s[rows, :])
        b_s[rows, :] = (jnp.dot(tri, hi, preferred_element_type=F32)
                        + jnp.dot(tri, mid, preferred_element_type=F32)
                        + jnp.dot(tri, lo, preferred_element_type=F32))

    trow = lax.broadcasted_iota(jnp.int32, (c, RNN_DK), 0)

    def phase1(n, carry):
        rows = pl.ds(pl.multiple_of(n * c, c), c)
        b = b_s[rows, :]
        q = q_s[rows, :]
        k = k_s[rows, :]
        v = zi_ref[rows, :]
        b_last = b[c - 1:c, :]
        o = jnp.zeros((c, RNN_DV), F32)
        for s in range(c):
            keep = trow >= s
            dl = jnp.where(keep, b - b[s:s + 1, :], 0.0)
            pw = jnp.where(keep, q * k[s:s + 1, :] * jnp.exp(dl), 0.0)
            o = o + jnp.sum(pw, axis=-1, keepdims=True) * v[s:s + 1, :]
        oi_s[rows, :] = o
        q_s[rows, :] = q * jnp.exp(b)
        kt = k * jnp.exp(b_last - b)
        st_s[n] = lax.dot_general(v.astype(BF16), kt.astype(BF16), TN_DIMS, preferred_element_type=F32)
        dec_s[pl.ds(n, 1), :] = jnp.exp(b_last)
        return carry

    lax.fori_loop(0, n_chunks, phase1, 0)

    def phase2(n, st):
        inc = st_s[n]
        st_s[n] = st
        return st * dec_s[pl.ds(n, 1), :] + inc

    st_fin = lax.fori_loop(0, n_chunks, phase2, s0_ref[...].T)
    sfin_ref[...] = st_fin.T

    def phase3(n, carry):
        rows = pl.ds(pl.multiple_of(n * c, c), c)
        inter = lax.dot_general(q_s[rows, :].astype(BF16), st_s[n].astype(BF16), NT_DIMS,
                                preferred_element_type=F32)
        o = oi_s[rows, :] + inter
        ms = jnp.mean(o * o, axis=-1, keepdims=True)
        zg = zg_ref[rows, :]
        o_ref[rows, :] = (o * lax.rsqrt(ms + NORM_EPS) * nw_ref[...] * (zg * _sigmoid(zg))).astype(o_ref.dtype)
        return carry

    lax.fori_loop(0, n_chunks, phase3, 0)


def hgrn_prompt(z, lb_param, s0, rnn_norm_w, batch, seq, col0):
    m = z.shape[0]
    heads = s0.shape[1]
    zblk = lambda g: pl.BlockSpec((seq, RNN_DK), lambda b, h, g=g: (b, col0 + g * heads + h))
    n_chunks = seq // HGRN_CHUNK
    return pl.pallas_call(
        functools.partial(_hgrn_prompt_kernel, seq=seq),
        grid=(batch, heads),
        in_specs=[zblk(0), zblk(1), zblk(2), zblk(3),
                  pl.BlockSpec((lb_param.shape[0], RNN_DK), lambda b, h: (0, h)),
                  pl.BlockSpec((None, None, RNN_DK, RNN_DV), lambda b, h: (b, h, 0, 0)),
                  pl.BlockSpec((1, RNN_DV), lambda b, h: (0, 0))],
        out_specs=[pl.BlockSpec((seq, RNN_DV), lambda b, h: (b, h)),
                   pl.BlockSpec((None, None, RNN_DK, RNN_DV), lambda b, h: (b, h, 0, 0))],
        out_shape=[jax.ShapeDtypeStruct((m, heads * RNN_DV), BF16),
                   jax.ShapeDtypeStruct(s0.shape, F32)],
        scratch_shapes=[pltpu.VMEM((seq, RNN_DK), F32), pltpu.VMEM((seq, RNN_DK), F32),
                        pltpu.VMEM((seq, RNN_DK), F32), pltpu.VMEM((seq, RNN_DV), F32),
                        pltpu.VMEM((n_chunks, RNN_DV, RNN_DK), F32),
                        pltpu.VMEM((n_chunks, RNN_DK), F32)],
        compiler_params=_params("parallel", "parallel"),
        name="hgrn_prompt",
    )(z, z, z, z, lb_param, s0, rnn_norm_w.reshape(1, RNN_DV))


def _hgrn_sample_kernel(zq_ref, zf_ref, zi_ref, zg_ref, lb_ref, s0_ref, nw_ref, o_ref, s_ref):
    bsz = s0_ref.shape[0]
    lb = _lower_bound(lb_ref)
    zf = zf_ref[...]
    f_t = (lb + (1.0 - lb) * _sigmoid(zf)).T
    k_t = ((1.0 - lb) * _sigmoid(-zf)).T
    zq = zq_ref[...]
    q_t = (zq * _sigmoid(zq)).T
    for b in range(bsz):
        v = zi_ref[b:b + 1, :]
        s_new = f_t[:, b:b + 1] * s0_ref[b] + k_t[:, b:b + 1] * v
        s_ref[b] = s_new
        o = jnp.sum(q_t[:, b:b + 1] * s_new, axis=0, keepdims=True)
        ms = jnp.mean(o * o, axis=-1, keepdims=True)
        zg = zg_ref[b:b + 1, :]
        o_ref[b:b + 1, :] = (o * lax.rsqrt(ms + NORM_EPS) * nw_ref[...] * (zg * _sigmoid(zg))).astype(o_ref.dtype)


def hgrn_sample(z, lb_param, s0, rnn_norm_w, col0):
    bsz, heads = s0.shape[:2]
    zblk = lambda g: pl.BlockSpec((bsz, RNN_DK), lambda h, g=g: (0, col0 + g * heads + h))
    return pl.pallas_call(
        _hgrn_sample_kernel,
        grid=(heads,),
        in_specs=[zblk(0), zblk(1), zblk(2), zblk(3),
                  pl.BlockSpec((lb_param.shape[0], RNN_DK), lambda h: (0, h)),
                  pl.BlockSpec((bsz, None, RNN_DK, RNN_DV), lambda h: (0, h, 0, 0)),
                  pl.BlockSpec((1, RNN_DV), lambda h: (0, 0))],
        out_specs=[pl.BlockSpec((bsz, RNN_DV), lambda h: (0, h)),
                   pl.BlockSpec((bsz, None, RNN_DK, RNN_DV), lambda h: (0, h, 0, 0))],
        out_shape=[jax.ShapeDtypeStruct((bsz, heads * RNN_DV), BF16),
                   jax.ShapeDtypeStruct(s0.shape, F32)],
        compiler_params=_params("parallel"),
        name="hgrn_sample",
    )(z, z, z, z, lb_param, s0, rnn_norm_w.reshape(1, RNN_DV))


def _top16(s, ids):
    big = jnp.int32(2 ** 30)
    vals, sel = [], []
    for _ in range(PEER_TOPK):
        m = jnp.max(s, axis=0, keepdims=True)
        pick = jnp.min(jnp.where(s == m, ids, big), axis=0, keepdims=True)
        s = jnp.where(ids == pick, -jnp.inf, s)
        vals.append(m)
        sel.append(pick)
    return jnp.concatenate(vals, axis=0), jnp.concatenate(sel, axis=0)


def _peer_topk_kernel(q_ref, sk_ref, a_ref, b_ref, g_ref):
    tm = q_ref.shape[0]
    heads = sk_ref.shape[0]
    key_ids = lax.broadcasted_iota(jnp.int32, (N_KEYS, tm), 0)
    k = PEER_TOPK
    for h in range(heads):
        half = []
        for c in range(2):
            col = (2 * h + c) * HEAD_DIM
            sc = lax.dot_general(sk_ref[h, c].astype(BF16), q_ref[:, col:col + HEAD_DIM].astype(BF16),
                                 NT_DIMS, preferred_element_type=F32)
            half.append(_top16(sc, key_ids))
        (s1, i1), (s2, i2) = half
        cand, eid, fid = [], [], []
        crow = lax.broadcasted_iota(jnp.int32, (k, tm), 0)
        for r in range(k // 2):
            nc = k if r == 0 else k // 2
            cand.append(s1[r:r + 1, :] + s2[:nc, :])
            eid.append(i1[r:r + 1, :] * N_KEYS + i2[:nc, :])
            fid.append(r * k + crow[:nc, :])
        cand.append(s1[k // 2:, :] + s2[0:1, :])
        eid.append(i1[k // 2:, :] * N_KEYS + i2[0:1, :])
        fid.append((k // 2 + crow[:k // 2, :]) * k)
        cand = jnp.concatenate(cand, axis=0)
        eid = jnp.concatenate(eid, axis=0)
        fid = jnp.concatenate(fid, axis=0)
        best, pick = _top16(cand, fid)
        e_sel = []
        for j in range(k):
            e_sel.append(jnp.max(jnp.where(fid == pick[j:j + 1, :], eid, -1), axis=0, keepdims=True))
        e_sel = jnp.concatenate(e_sel, axis=0)
        p = jnp.exp(best - best[0:1, :])
        gate = p / jnp.sum(p, axis=0, keepdims=True)
        rows = slice(h * k, (h + 1) * k)
        a_ref[rows, :] = jnp.right_shift(e_sel, LOG2_LANES)
        b_ref[rows, :] = jnp.bitwise_and(e_sel, N_KEYS - 1)
        g_ref[rows, :] = gate


def peer_topk(q, subkeys, tm):
    m = q.shape[0]
    heads = subkeys.shape[0]
    n_sel = heads * PEER_TOPK
    out = pl.BlockSpec((n_sel, tm), lambda i: (0, i))
    return pl.pallas_call(
        _peer_topk_kernel,
        grid=(m // tm,),
        in_specs=[pl.BlockSpec((tm, q.shape[1]), lambda i: (i, 0)),
                  pl.BlockSpec(subkeys.shape, lambda i: (0, 0, 0, 0))],
        out_specs=[out, out, out],
        out_shape=[jax.ShapeDtypeStruct((n_sel, m), jnp.int32),
                   jax.ShapeDtypeStruct((n_sel, m), jnp.int32),
                   jax.ShapeDtypeStruct((n_sel, m), F32)],
        compiler_params=_params("parallel"),
        name="peer_topk",
    )(q, subkeys)


def _peer_gates_kernel(a_ref, b_ref, g_ref, o_ref):
    tg, n_sel = a_ref.shape
    ids = lax.broadcasted_iota(jnp.int32, (N_KEYS, n_sel), 0)

    def body(t, carry):
        row = pl.ds(t, 1)
        left = jnp.where(ids == a_ref[row, :], 1.0, 0.0).astype(BF16)
        right = jnp.where(ids == b_ref[row, :], g_ref[row, :], 0.0).astype(BF16)
        o_ref[t] = lax.dot_general(left, right, NT_DIMS, preferred_element_type=F32).astype(o_ref.dtype)
        return carry

    lax.fori_loop(0, tg, body, 0)


def peer_gates(a, b, g, tg):
    m, n_sel = a.shape
    spec = pl.BlockSpec((tg, n_sel), lambda i: (i, 0))
    return pl.pallas_call(
        _peer_gates_kernel,
        grid=(m // tg,),
        in_specs=[spec, spec, spec],
        out_specs=pl.BlockSpec((tg, N_KEYS, N_KEYS), lambda i: (i, 0, 0)),
        out_shape=jax.ShapeDtypeStruct((m, N_KEYS, N_KEYS), F32),
        compiler_params=_params("parallel"),
        name="peer_gates",
    )(a, b, g)


def _gelu(x):
    return 0.5 * x * (1.0 + lax.erf(x * (2.0 ** -0.5)))


def _peer_dense_kernel(x_ref, u_ref, v_ref, gs_ref, h_hbm, y_ref, *, tm):
    i = pl.program_id(0)

    @pl.when(pl.program_id(1) == 0)
    def _():
        pltpu.sync_copy(h_hbm.at[pl.ds(i * tm, tm), :], y_ref)

    hid = lax.dot_general(x_ref[...], u_ref[...], NT_DIMS, preferred_element_type=F32)
    n_sub = hid.shape[1] // N_KEYS
    first = (pl.program_id(1) % (gs_ref.shape[1] // n_sub)) * n_sub
    w = []
    for j in range(n_sub):
        sl = slice(j * N_KEYS, (j + 1) * N_KEYS)
        w.append((gs_ref[:, first + j, :] * _gelu(hid[:, sl])).astype(BF16))
    w = jnp.concatenate(w, axis=-1)
    y_ref[...] += jnp.dot(w, v_ref[...], preferred_element_type=F32)


def peer_dense(xn, u_tab, v_tab, gates, h, tm, te):
    m, d = xn.shape
    n_exp = u_tab.shape[0]
    return pl.pallas_call(
        functools.partial(_peer_dense_kernel, tm=tm),
        grid=(m // tm, n_exp // te),
        in_specs=[pl.BlockSpec((tm, d), lambda i, e: (i, 0)),
                  pl.BlockSpec((te, d), lambda i, e: (e, 0)),
                  pl.BlockSpec((te, d), lambda i, e: (e, 0)),
                  pl.BlockSpec((tm, GATE_ROWS, N_KEYS), lambda i, e: (i, e // (GATE_ROWS * N_KEYS // te), 0)),
                  pl.BlockSpec(memory_space=pl.ANY)],
        out_specs=pl.BlockSpec((tm, d), lambda i, e: (i, 0)),
        out_shape=jax.ShapeDtypeStruct((m, d), F32),
        compiler_params=_params("parallel", "arbitrary"),
        name="peer_dense",
    )(xn, u_tab, v_tab, gates, h)


def _row_tile(m, cap):
    return min(m, cap)


def _trunk(x, attend, recur, w, lam_init):
    m, d = x.shape
    att_w = w["q_norm_cols"]
    xn = rmsnorm_bf16(x, w["norm1_w"], _row_tile(m, 256))
    z = matmul(xn, w["w_in"], tm=_row_tile(m, 1024), tn=512)
    qn, kn, knb, v, vb = qkv_post(z, w["q_norm_w"], w["k_norm_w"], att_w, _row_tile(m, 256))
    att = attend(qn, kn, knb, v, vb)
    rnn, s_fin = recur(z)
    mix = jnp.concatenate([att, rnn], axis=-1)
    h = matmul(mix, w["w_out"], x, tm=_row_tile(m, 1024), tn=512)
    hn = rmsnorm_bf16(h, w["norm2_w"], _row_tile(m, 256))
    pq = matmul(hn, w["peer_wq"], tm=_row_tile(m, 1024), tn=512)
    a, b, g = peer_topk(pq, w["peer_subkeys"], 128)
    gates = peer_gates(a.T, b.T, g.T, 64)
    y = peer_dense(hn, w["peer_u"], w["peer_v"], gates, h, _row_tile(m, 512), 512)
    return y, kn, v, s_fin


def kernel(x_prompt, x_sample, cache_k, cache_v, state_rnn, page_table, norm1_w, w_in, q_norm_w, k_norm_w, lambda_q1, lambda_k1, lambda_q2, lambda_k2, sub_norm_w, lb_param, rnn_norm_w, w_out, norm2_w, peer_wq, peer_subkeys, peer_u, peer_v):
    depth = w_in.shape[0]
    assert depth == 1, "single-layer trunk"
    layer = 0
    batch, seq, d = x_prompt.shape
    n_dec, dec_seq, _ = x_sample.shape
    assert dec_seq == 1
    h_att = cache_k.shape[3]
    att_w = h_att * VAL_DIM
    h_rnn = state_rnn.shape[2]
    lam_init = 0.8 - 0.6 * math.exp(-0.3 * layer)
    rnn_col0 = 3 * att_w // LANES

    w = dict(norm1_w=norm1_w[layer], w_in=w_in[layer], q_norm_w=q_norm_w[layer], k_norm_w=k_norm_w[layer],
             w_out=w_out[layer], norm2_w=norm2_w[layer], peer_wq=peer_wq[layer],
             peer_subkeys=peer_subkeys[layer], peer_u=peer_u[layer].astype(BF16),
             peer_v=peer_v[layer].astype(BF16), q_norm_cols=att_w)
    lams = [p[layer].reshape(1, HEAD_DIM) for p in (lambda_q1, lambda_k1, lambda_q2, lambda_k2)]
    sub_w = sub_norm_w[layer]
    rnn_w = rnn_norm_w[layer]

    s0 = jnp.zeros((batch, h_rnn, RNN_DK, RNN_DV), F32)
    y_p, k_p, v_p, s_p = _trunk(
        x_prompt.reshape(batch * seq, d),
        lambda qn, kn, knb, v, vb: attn_prompt(qn, knb, vb, lams, sub_w, batch, seq, lam_init, 256),
        lambda z: hgrn_prompt(z, lb_param, s0, rnn_w, batch, seq, rnn_col0),
        w, lam_init)

    m_pad = LANES
    xs = jnp.zeros((m_pad, d), F32).at[:n_dec].set(x_sample.reshape(n_dec, d))
    ck = cache_k[layer].reshape(cache_k.shape[1], PAGE_SIZE, att_w)
    cv = cache_v[layer].reshape(cache_v.shape[1], PAGE_SIZE, att_w)

    def attend_s(qn, kn, knb, v, vb):
        o = attn_sample(qn[:n_dec].reshape(n_dec, 1, att_w), kn[:n_dec].reshape(n_dec, 1, att_w),
                        v[:n_dec].reshape(n_dec, 1, att_w), ck, cv, page_table, lams, sub_w, lam_init)
        return jnp.zeros((m_pad, att_w), BF16).at[:n_dec].set(o.reshape(n_dec, att_w))

    def recur_s(z):
        o, s_new = hgrn_sample(z, lb_param, state_rnn[layer], rnn_w, rnn_col0)
        return jnp.zeros((m_pad, o.shape[1]), BF16).at[:n_dec].set(o), s_new

    y_s, k_s, v_s, s_s = _trunk(xs, attend_s, recur_s, w, lam_init)

    return (y_p.reshape(batch, seq, d),
            y_s[:n_dec].reshape(n_dec, 1, d),
            k_p.reshape(1, batch, seq, h_att, 2, HEAD_DIM),
            v_p.reshape(1, batch, seq, h_att, VAL_DIM),
            k_s[:n_dec].reshape(1, n_dec, 1, h_att, 2, HEAD_DIM),
            v_s[:n_dec].reshape(1, n_dec, 1, h_att, VAL_DIM),
            s_p[None].astype(state_rnn.dtype),
            s_s[None].astype(state_rnn.dtype))
```

```python
import functools
import math

import jax
import jax.numpy as jnp
from jax import lax
from jax.experimental import pallas as pl
from jax.experimental.pallas import tpu as pltpu

F32 = jnp.float32
BF16 = jnp.bfloat16

LANES = 128
LOG2_LANES = 7
HEAD_DIM = 128
VAL_DIM = 2 * HEAD_DIM
RNN_DK = 128
RNN_DV = 128
HGRN_CHUNK = 16
NORM_EPS = 1e-6
ATTN_SCALE = HEAD_DIM ** -0.5
N_KEYS = 128
PEER_TOPK = 16
PAGE_SIZE = 128
GATE_ROWS = 8
SAMPLE_PAGES_PER_STEP = 4
VMEM_LIMIT = 56 * 1024 * 1024

NT_DIMS = (((1,), (1,)), ((), ()))
TN_DIMS = (((0,), (0,)), ((), ()))


def _params(*sem):
    return pltpu.CompilerParams(dimension_semantics=sem, vmem_limit_bytes=VMEM_LIMIT)


def _sigmoid(x):
    return 1.0 / (1.0 + jnp.exp(-x))


def _rmsnorm_kernel(x_ref, w_ref, o_ref):
    x = x_ref[...]
    ms = jnp.mean(x * x, axis=-1, keepdims=True)
    o_ref[...] = (x * lax.rsqrt(ms + NORM_EPS) * w_ref[...]).astype(o_ref.dtype)


def rmsnorm_bf16(x, w, tm):
    m, d = x.shape
    return pl.pallas_call(
        _rmsnorm_kernel,
        grid=(m // tm,),
        in_specs=[pl.BlockSpec((tm, d), lambda i: (i, 0)),
                  pl.BlockSpec((1, d), lambda i: (0, 0))],
        out_specs=pl.BlockSpec((tm, d), lambda i: (i, 0)),
        out_shape=jax.ShapeDtypeStruct((m, d), BF16),
        compiler_params=_params("parallel"),
        name="rmsnorm",
    )(x, w.reshape(1, d))


def _matmul_kernel(*refs, has_res):
    if has_res:
        x_ref, w_ref, r_ref, o_ref, wb_ref = refs
    else:
        x_ref, w_ref, o_ref, wb_ref = refs

    @pl.when(pl.program_id(1) == 0)
    def _():
        wb_ref[...] = w_ref[...].astype(BF16)

    acc = jnp.dot(x_ref[...], wb_ref[...], preferred_element_type=F32)
    if has_res:
        acc = r_ref[...] + acc
    o_ref[...] = acc


def matmul(x, w, res=None, *, tm, tn):
    m, k = x.shape
    n = w.shape[1]
    in_specs = [pl.BlockSpec((tm, k), lambda j, i: (i, 0)),
                pl.BlockSpec((k, tn), lambda j, i: (0, j))]
    args = [x, w]
    if res is not None:
        in_specs.append(pl.BlockSpec((tm, tn), lambda j, i: (i, j)))
        args.append(res)
    return pl.pallas_call(
        functools.partial(_matmul_kernel, has_res=res is not None),
        grid=(n // tn, m // tm),
        in_specs=in_specs,
        out_specs=pl.BlockSpec((tm, tn), lambda j, i: (i, j)),
        out_shape=jax.ShapeDtypeStruct((m, n), F32),
        scratch_shapes=[pltpu.VMEM((k, tn), BF16)],
        compiler_params=_params("arbitrary", "arbitrary"),
        name="matmul",
    )(*args)


def _qkv_kernel(zq_ref, zk_ref, zv_ref, qw_ref, kw_ref, qn_ref, kn_ref, knb_ref, v_ref, vb_ref):
    width = zq_ref.shape[1]
    for g in range(width // HEAD_DIM):
        sl = slice(g * HEAD_DIM, (g + 1) * HEAD_DIM)
        xq = zq_ref[:, sl]
        yq = xq * lax.rsqrt(jnp.mean(xq * xq, axis=-1, keepdims=True) + NORM_EPS) * qw_ref[...]
        qn_ref[:, sl] = yq.astype(BF16)
        xk = zk_ref[:, sl]
        yk = xk * lax.rsqrt(jnp.mean(xk * xk, axis=-1, keepdims=True) + NORM_EPS) * kw_ref[...]
        kn_ref[:, sl] = yk
        knb_ref[:, sl] = yk.astype(BF16)
    v = zv_ref[...]
    v_ref[...] = v
    vb_ref[...] = v.astype(BF16)


def qkv_post(z, q_norm_w, k_norm_w, width, tm):
    m = z.shape[0]
    blk = lambda c: pl.BlockSpec((tm, width), lambda i, c=c: (i, c))
    wspec = pl.BlockSpec((1, HEAD_DIM), lambda i: (0, 0))
    out = lambda dt: jax.ShapeDtypeStruct((m, width), dt)
    return pl.pallas_call(
        _qkv_kernel,
        grid=(m // tm,),
        in_specs=[blk(0), blk(1), blk(2), wspec, wspec],
        out_specs=[blk(0)] * 5,
        out_shape=[out(BF16), out(F32), out(BF16), out(F32), out(BF16)],
        compiler_params=_params("parallel"),
        name="qkv_post",
    )(z, z, z, q_norm_w.reshape(1, HEAD_DIM), k_norm_w.reshape(1, HEAD_DIM))


def _lambda(lq1_ref, lk1_ref, lq2_ref, lk2_ref, lam_init):
    a = jnp.sum(lq1_ref[...] * lk1_ref[...], axis=-1, keepdims=True)
    b = jnp.sum(lq2_ref[...] * lk2_ref[...], axis=-1, keepdims=True)
    return jnp.exp(a) - jnp.exp(b) + lam_init


def _sub_norm(o, w, lam_init):
    ms = jnp.mean(o * o, axis=-1, keepdims=True)
    return o * lax.rsqrt(ms + NORM_EPS) * w * (1.0 - lam_init)


def _attn_prompt_kernel(q_ref, k_ref, v_ref, lq1_ref, lk1_ref, lq2_ref, lk2_ref, sw_ref, o_ref,
                        m_ref, l_ref, acc_ref, *, tq, lam_init):
    qi = pl.program_id(2)
    m_ref[...] = jnp.full(m_ref.shape, -jnp.inf, F32)
    l_ref[...] = jnp.zeros(l_ref.shape, F32)
    acc_ref[...] = jnp.zeros(acc_ref.shape, F32)
    tk = 2 * tq
    row = qi * tq + lax.broadcasted_iota(jnp.int32, (tq, tk), 0)

    def body(j, carry):
        kv_rows = pl.ds(pl.multiple_of(j * tk, tk), tk)
        col = j * tk + lax.broadcasted_iota(jnp.int32, (tq, tk), 1)
        keep = col <= row
        vblk = v_ref[kv_rows, :]
        for c in range(2):
            sl = slice(c * HEAD_DIM, (c + 1) * HEAD_DIM)
            s = lax.dot_general(q_ref[:, sl], k_ref[kv_rows, sl], NT_DIMS,
                                preferred_element_type=F32) * ATTN_SCALE
            s = jnp.where(keep, s, -jnp.inf)
            m_old = m_ref[c]
            m_new = jnp.maximum(m_old, jnp.max(s, axis=-1, keepdims=True))
            alpha = jnp.exp(m_old - m_new)
            p = jnp.exp(s - m_new)
            l_ref[c] = alpha * l_ref[c] + jnp.sum(p, axis=-1, keepdims=True)
            acc_ref[c] = alpha * acc_ref[c] + jnp.dot(p.astype(BF16), vblk, preferred_element_type=F32)
            m_ref[c] = m_new
        return carry

    lax.fori_loop(0, qi // 2 + 1, body, 0)
    lam = _lambda(lq1_ref, lk1_ref, lq2_ref, lk2_ref, lam_init)
    o = acc_ref[0] / l_ref[0] - lam * (acc_ref[1] / l_ref[1])
    o_ref[...] = _sub_norm(o, sw_ref[...], lam_init).astype(o_ref.dtype)


def attn_prompt(qn, knb, vb, lams, sub_norm_w, batch, seq, lam_init, tq):
    m, width = qn.shape
    heads = width // VAL_DIM
    nq = seq // tq
    vec = pl.BlockSpec((1, HEAD_DIM), lambda b, h, i: (0, 0))
    return pl.pallas_call(
        functools.partial(_attn_prompt_kernel, tq=tq, lam_init=lam_init),
        grid=(batch, heads, nq),
        in_specs=[pl.BlockSpec((tq, VAL_DIM), lambda b, h, i: (b * nq + i, h)),
                  pl.BlockSpec((seq, VAL_DIM), lambda b, h, i: (b, h)),
                  pl.BlockSpec((seq, VAL_DIM), lambda b, h, i: (b, h)),
                  vec, vec, vec, vec,
                  pl.BlockSpec((1, VAL_DIM), lambda b, h, i: (0, 0))],
        out_specs=pl.BlockSpec((tq, VAL_DIM), lambda b, h, i: (b * nq + i, h)),
        out_shape=jax.ShapeDtypeStruct((m, width), BF16),
        scratch_shapes=[pltpu.VMEM((2, tq, 1), F32), pltpu.VMEM((2, tq, 1), F32),
                        pltpu.VMEM((2, tq, VAL_DIM), F32)],
        compiler_params=_params("parallel", "parallel", "arbitrary"),
        name="attn_prompt",
    )(qn, knb, vb, *lams, sub_norm_w.reshape(1, VAL_DIM))


def _attn_sample_kernel(pt_ref, q_ref, kn_ref, vn_ref, lq1_ref, lk1_ref, lq2_ref, lk2_ref, sw_ref, *rest,
                        n_grp, lam_init):
    del pt_ref
    k_refs, v_refs = rest[:n_grp], rest[n_grp:2 * n_grp]
    o_ref, m_ref, l_ref, acc_ref = rest[2 * n_grp:]
    step = pl.program_id(1)
    heads = q_ref.shape[1]

    @pl.when(step == 0)
    def _():
        m_ref[...] = jnp.full(m_ref.shape, -jnp.inf, F32)
        l_ref[...] = jnp.zeros(l_ref.shape, F32)
        acc_ref[...] = jnp.zeros(acc_ref.shape, F32)

    q = [q_ref[c] * ATTN_SCALE for c in range(2)]
    for c in range(2):
        scores = []
        for g in range(n_grp):
            kc = k_refs[g][pl.ds(c, PAGE_SIZE * heads, stride=2), :].reshape(PAGE_SIZE, heads, HEAD_DIM)
            scores.append(jnp.sum(kc * q[c][None], axis=-1, keepdims=True))
        m_old = m_ref[c]
        m_new = m_old
        for s in scores:
            m_new = jnp.maximum(m_new, jnp.max(s, axis=0))
        alpha = jnp.exp(m_old - m_new)
        l_new = alpha * l_ref[c]
        acc = alpha * acc_ref[c]
        for g in range(n_grp):
            p = jnp.exp(scores[g] - m_new[None])
            l_new = l_new + jnp.sum(p, axis=0)
            acc = acc + jnp.sum(p * v_refs[g][...], axis=0)
        m_ref[c] = m_new
        l_ref[c] = l_new
        acc_ref[c] = acc

    @pl.when(step == pl.num_programs(1) - 1)
    def _():
        outs = []
        for c in range(2):
            s_new = jnp.sum(kn_ref[c] * q[c], axis=-1, keepdims=True)
            m_old = m_ref[c]
            m_new = jnp.maximum(m_old, s_new)
            alpha = jnp.exp(m_old - m_new)
            p = jnp.exp(s_new - m_new)
            outs.append((alpha * acc_ref[c] + p * vn_ref[...]) / (alpha * l_ref[c] + p))
        lam = _lambda(lq1_ref, lk1_ref, lq2_ref, lk2_ref, lam_init)
        o_ref[...] = _sub_norm(outs[0] - lam * outs[1], sw_ref[...], lam_init).astype(o_ref.dtype)


def attn_sample(q, kn, vn, cache_k, cache_v, page_table, lams, sub_norm_w, lam_init, n_grp):
    bsz, _, heads, _ = q.shape
    n_pages = page_table.shape[1]
    qspec = pl.BlockSpec((None, 2, heads, HEAD_DIM), lambda b, s, pt: (b, 0, 0, 0))
    vspec = pl.BlockSpec((None, heads, VAL_DIM), lambda b, s, pt: (b, 0, 0))
    vec = pl.BlockSpec((1, HEAD_DIM), lambda b, s, pt: (0, 0))

    def page(shape, g):
        zeros = (0,) * (len(shape) - 1)
        return pl.BlockSpec((None,) + tuple(shape[1:]),
                            lambda b, s, pt: (pt[b * n_pages + s * n_grp + g],) + zeros)

    grid_spec = pltpu.PrefetchScalarGridSpec(
        num_scalar_prefetch=1,
        grid=(bsz, n_pages // n_grp),
        in_specs=[qspec, qspec, vspec, vec, vec, vec, vec, pl.BlockSpec((1, VAL_DIM), lambda b, s, pt: (0, 0))]
        + [page(cache_k.shape, g) for g in range(n_grp)] + [page(cache_v.shape, g) for g in range(n_grp)],
        out_specs=vspec,
        scratch_shapes=[pltpu.VMEM((2, heads, 1), F32), pltpu.VMEM((2, heads, 1), F32),
                        pltpu.VMEM((2, heads, VAL_DIM), F32)],
    )
    return pl.pallas_call(
        functools.partial(_attn_sample_kernel, n_grp=n_grp, lam_init=lam_init),
        grid_spec=grid_spec,
        out_shape=jax.ShapeDtypeStruct((bsz, heads, VAL_DIM), BF16),
        compiler_params=_params("parallel", "arbitrary"),
        name="attn_sample",
    )(page_table.reshape(-1), q, kn, vn, *lams, sub_norm_w.reshape(1, VAL_DIM),
      *([cache_k] * n_grp), *([cache_v] * n_grp))


def _lower_bound(lb_ref):
    p = lb_ref[...]
    e = jnp.exp(p - jnp.max(p, axis=0, keepdims=True))
    return e[0:1, :] / jnp.sum(e, axis=0, keepdims=True)


def _split3(x):
    hi = x.astype(BF16)
    r = x - hi.astype(F32)
    mid = r.astype(BF16)
    lo = (r - mid.astype(F32)).astype(BF16)
    return hi, mid, lo


def _hgrn_prompt_kernel(zq_ref, zf_ref, zi_ref, zg_ref, lb_ref, s0_ref, nw_ref, o_ref, sfin_ref,
                        q_s, k_s, b_s, oi_s, st_s, dec_s, *, seq):
    c = HGRN_CHUNK
    n_chunks = seq // c
    lb = _lower_bound(lb_ref)
    zf = zf_ref[...]
    f = lb + (1.0 - lb) * _sigmoid(zf)
    k_s[...] = (1.0 - lb) * _sigmoid(-zf)
    zq = zq_ref[...]
    q_s[...] = zq * _sigmoid(zq)
    logf = jnp.log(f)

    grp = 256
    r = lax.broadcasted_iota(jnp.int32, (grp, grp), 0)
    cc = lax.broadcasted_iota(jnp.int32, (grp, grp), 1)
    shift = c.bit_length() - 1
    same_chunk = jnp.right_shift(r, shift) == jnp.right_shift(cc, shift)
    tri = jnp.where(same_chunk & (cc <= r), 1.0, 0.0).astype(BF16)
    b_s[...] = logf
    for gi in range(seq // grp):
        rows = slice(gi * grp, (gi + 1) * grp)
        hi, mid, lo = _split3(b_s[rows, :])
        b_s[rows, :] = (jnp.dot(tri, hi, preferred_element_type=F32)
                        + jnp.dot(tri, mid, preferred_element_type=F32)
                        + jnp.dot(tri, lo, preferred_element_type=F32))

    half = c // 2
    hrow = lax.broadcasted_iota(jnp.int32, (half, RNN_DK), 0)

    def phase1(n, carry):
        rows = pl.ds(pl.multiple_of(n * c, c), c)
        b = b_s[rows, :]
        q = q_s[rows, :]
        k = k_s[rows, :]
        v = zi_ref[rows, :]
        b_last = b[c - 1:c, :]
        b_h, q_h = (b[:half], b[half:]), (q[:half], q[half:])
        o_h = [jnp.zeros((half, RNN_DV), F32), jnp.zeros((half, RNN_DV), F32)]
        for s in range(c):
            bs, ks, vs = b[s:s + 1, :], k[s:s + 1, :], v[s:s + 1, :]
            own = s // half
            keep = hrow >= s - own * half
            pw = jnp.where(keep, q_h[own] * ks * jnp.exp(jnp.where(keep, b_h[own] - bs, 0.0)), 0.0)
            o_h[own] = o_h[own] + jnp.sum(pw, axis=-1, keepdims=True) * vs
            if own == 0:
                pw = q_h[1] * ks * jnp.exp(b_h[1] - bs)
                o_h[1] = o_h[1] + jnp.sum(pw, axis=-1, keepdims=True) * vs
        oi_s[rows, :] = jnp.concatenate(o_h, axis=0)
        q_s[rows, :] = q * jnp.exp(b)
        kt = k * jnp.exp(b_last - b)
        st_s[n] = lax.dot_general(v.astype(BF16), kt.astype(BF16), TN_DIMS, preferred_element_type=F32)
        dec_s[pl.ds(n, 1), :] = jnp.exp(b_last)
        return carry

    lax.fori_loop(0, n_chunks, phase1, 0, unroll=2)

    def phase2(n, st):
        inc = st_s[n]
        st_s[n] = st
        return st * dec_s[pl.ds(n, 1), :] + inc

    st_fin = lax.fori_loop(0, n_chunks, phase2, s0_ref[...].T)
    sfin_ref[...] = st_fin.T

    def phase3(n, carry):
        rows = pl.ds(pl.multiple_of(n * c, c), c)
        oi_s[rows, :] += lax.dot_general(q_s[rows, :].astype(BF16), st_s[n].astype(BF16), NT_DIMS,
                                         preferred_element_type=F32)
        return carry

    lax.fori_loop(0, n_chunks, phase3, 0, unroll=8)
    o = oi_s[...]
    ms = jnp.mean(o * o, axis=-1, keepdims=True)
    zg = zg_ref[...]
    o_ref[...] = (o * lax.rsqrt(ms + NORM_EPS) * nw_ref[...] * (zg * _sigmoid(zg))).astype(o_ref.dtype)


def hgrn_prompt(z, lb_param, s0, rnn_norm_w, batch, seq, col0):
    m = z.shape[0]
    heads = s0.shape[1]
    zblk = lambda g: pl.BlockSpec((seq, RNN_DK), lambda b, h, g=g: (b, col0 + g * heads + h))
    n_chunks = seq // HGRN_CHUNK
    return pl.pallas_call(
        functools.partial(_hgrn_prompt_kernel, seq=seq),
        grid=(batch, heads),
        in_specs=[zblk(0), zblk(1), zblk(2), zblk(3),
                  pl.BlockSpec((lb_param.shape[0], RNN_DK), lambda b, h: (0, h)),
                  pl.BlockSpec((None, None, RNN_DK, RNN_DV), lambda b, h: (b, h, 0, 0)),
                  pl.BlockSpec((1, RNN_DV), lambda b, h: (0, 0))],
        out_specs=[pl.BlockSpec((seq, RNN_DV), lambda b, h: (b, h)),
                   pl.BlockSpec((None, None, RNN_DK, RNN_DV), lambda b, h: (b, h, 0, 0))],
        out_shape=[jax.ShapeDtypeStruct((m, heads * RNN_DV), BF16),
                   jax.ShapeDtypeStruct(s0.shape, F32)],
        scratch_shapes=[pltpu.VMEM((seq, RNN_DK), F32), pltpu.VMEM((seq, RNN_DK), F32),
                        pltpu.VMEM((seq, RNN_DK), F32), pltpu.VMEM((seq, RNN_DV), F32),
                        pltpu.VMEM((n_chunks, RNN_DV, RNN_DK), F32),
                        pltpu.VMEM((n_chunks, RNN_DK), F32)],
        compiler_params=_params("parallel", "parallel"),
        name="hgrn_prompt",
    )(z, z, z, z, lb_param, s0, rnn_norm_w.reshape(1, RNN_DV))


def _hgrn_sample_kernel(zq_ref, zf_ref, zi_ref, zg_ref, lb_ref, s0_ref, nw_ref, o_ref, s_ref):
    bsz = s0_ref.shape[0]
    lb = _lower_bound(lb_ref)
    zf = zf_ref[...]
    f_t = (lb + (1.0 - lb) * _sigmoid(zf)).T
    k_t = ((1.0 - lb) * _sigmoid(-zf)).T
    zq = zq_ref[...]
    q_t = (zq * _sigmoid(zq)).T
    for b in range(bsz):
        v = zi_ref[b:b + 1, :]
        s_new = f_t[:, b:b + 1] * s0_ref[b] + k_t[:, b:b + 1] * v
        s_ref[b] = s_new
        o = jnp.sum(q_t[:, b:b + 1] * s_new, axis=0, keepdims=True)
        ms = jnp.mean(o * o, axis=-1, keepdims=True)
        zg = zg_ref[b:b + 1, :]
        o_ref[b:b + 1, :] = (o * lax.rsqrt(ms + NORM_EPS) * nw_ref[...] * (zg * _sigmoid(zg))).astype(o_ref.dtype)


def hgrn_sample(z, lb_param, s0, rnn_norm_w, col0):
    bsz, heads = s0.shape[:2]
    zblk = lambda g: pl.BlockSpec((bsz, RNN_DK), lambda h, g=g: (0, col0 + g * heads + h))
    return pl.pallas_call(
        _hgrn_sample_kernel,
        grid=(heads,),
        in_specs=[zblk(0), zblk(1), zblk(2), zblk(3),
                  pl.BlockSpec((lb_param.shape[0], RNN_DK), lambda h: (0, h)),
                  pl.BlockSpec((bsz, None, RNN_DK, RNN_DV), lambda h: (0, h, 0, 0)),
                  pl.BlockSpec((1, RNN_DV), lambda h: (0, 0))],
        out_specs=[pl.BlockSpec((bsz, RNN_DV), lambda h: (0, h)),
                   pl.BlockSpec((bsz, None, RNN_DK, RNN_DV), lambda h: (0, h, 0, 0))],
        out_shape=[jax.ShapeDtypeStruct((bsz, heads * RNN_DV), BF16),
                   jax.ShapeDtypeStruct(s0.shape, F32)],
        compiler_params=_params("parallel"),
        name="hgrn_sample",
    )(z, z, z, z, lb_param, s0, rnn_norm_w.reshape(1, RNN_DV))


def _top16(s, ids):
    big = jnp.int32(2 ** 30)
    vals, sel = [], []
    for _ in range(PEER_TOPK):
        m = jnp.max(s, axis=0, keepdims=True)
        pick = jnp.min(jnp.where(s == m, ids, big), axis=0, keepdims=True)
        s = jnp.where(ids == pick, -jnp.inf, s)
        vals.append(m)
        sel.append(pick)
    return jnp.concatenate(vals, axis=0), jnp.concatenate(sel, axis=0)


def _peer_topk_kernel(q_ref, sk_ref, a_ref, b_ref, g_ref):
    tm = q_ref.shape[0]
    heads = sk_ref.shape[0]
    key_ids = lax.broadcasted_iota(jnp.int32, (N_KEYS, tm), 0)
    k = PEER_TOPK
    for h in range(heads):
        half = []
        for c in range(2):
            col = (2 * h + c) * HEAD_DIM
            sc = lax.dot_general(sk_ref[h, c].astype(BF16), q_ref[:, col:col + HEAD_DIM].astype(BF16),
                                 NT_DIMS, preferred_element_type=F32)
            half.append(_top16(sc, key_ids))
        (s1, i1), (s2, i2) = half
        cand, eid, fid = [], [], []
        crow = lax.broadcasted_iota(jnp.int32, (k, tm), 0)
        for r in range(k // 2):
            nc = k if r == 0 else k // 2
            cand.append(s1[r:r + 1, :] + s2[:nc, :])
            eid.append(i1[r:r + 1, :] * N_KEYS + i2[:nc, :])
            fid.append(r * k + crow[:nc, :])
        cand.append(s1[k // 2:, :] + s2[0:1, :])
        eid.append(i1[k // 2:, :] * N_KEYS + i2[0:1, :])
        fid.append((k // 2 + crow[:k // 2, :]) * k)
        cand = jnp.concatenate(cand, axis=0)
        eid = jnp.concatenate(eid, axis=0)
        fid = jnp.concatenate(fid, axis=0)
        best, pick = _top16(cand, fid)
        e_sel = []
        for j in range(k):
            e_sel.append(jnp.max(jnp.where(fid == pick[j:j + 1, :], eid, -1), axis=0, keepdims=True))
        e_sel = jnp.concatenate(e_sel, axis=0)
        p = jnp.exp(best - best[0:1, :])
        gate = p / jnp.sum(p, axis=0, keepdims=True)
        rows = slice(h * k, (h + 1) * k)
        a_ref[rows, :] = jnp.right_shift(e_sel, LOG2_LANES)
        b_ref[rows, :] = jnp.bitwise_and(e_sel, N_KEYS - 1)
        g_ref[rows, :] = gate


def peer_topk(q, subkeys, tm):
    m = q.shape[0]
    heads = subkeys.shape[0]
    n_sel = heads * PEER_TOPK
    out = pl.BlockSpec((n_sel, tm), lambda i: (0, i))
    return pl.pallas_call(
        _peer_topk_kernel,
        grid=(m // tm,),
        in_specs=[pl.BlockSpec((tm, q.shape[1]), lambda i: (i, 0)),
                  pl.BlockSpec(subkeys.shape, lambda i: (0, 0, 0, 0))],
        out_specs=[out, out, out],
        out_shape=[jax.ShapeDtypeStruct((n_sel, m), jnp.int32),
                   jax.ShapeDtypeStruct((n_sel, m), jnp.int32),
                   jax.ShapeDtypeStruct((n_sel, m), F32)],
        compiler_params=_params("parallel"),
        name="peer_topk",
    )(q, subkeys)


def _peer_gates_kernel(a_ref, b_ref, g_ref, o_ref):
    tg, n_sel = a_ref.shape
    ids = lax.broadcasted_iota(jnp.int32, (N_KEYS, n_sel), 0)

    def body(t, carry):
        row = pl.ds(t, 1)
        left = jnp.where(ids == a_ref[row, :], 1.0, 0.0).astype(BF16)
        right = jnp.where(ids == b_ref[row, :], g_ref[row, :], 0.0).astype(BF16)
        o_ref[t] = lax.dot_general(left, right, NT_DIMS, preferred_element_type=F32).astype(o_ref.dtype)
        return carry

    lax.fori_loop(0, tg, body, 0, unroll=8)


def peer_gates(a, b, g, tg):
    m, n_sel = a.shape
    spec = pl.BlockSpec((tg, n_sel), lambda i: (i, 0))
    return pl.pallas_call(
        _peer_gates_kernel,
        grid=(m // tg,),
        in_specs=[spec, spec, spec],
        out_specs=pl.BlockSpec((tg, N_KEYS, N_KEYS), lambda i: (i, 0, 0)),
        out_shape=jax.ShapeDtypeStruct((m, N_KEYS, N_KEYS), F32),
        compiler_params=_params("parallel"),
        name="peer_gates",
    )(a, b, g)


def _gelu(x):
    return 0.5 * x * (1.0 + lax.erf(x * (2.0 ** -0.5)))


def _peer_dense_kernel(x_ref, u_ref, v_ref, gs_ref, h_hbm, y_ref, *, tm):
    i = pl.program_id(0)

    @pl.when(pl.program_id(1) == 0)
    def _():
        pltpu.sync_copy(h_hbm.at[pl.ds(i * tm, tm), :], y_ref)

    hid = lax.dot_general(x_ref[...], u_ref[...], NT_DIMS, preferred_element_type=F32)
    n_sub = hid.shape[1] // N_KEYS
    first = (pl.program_id(1) % (gs_ref.shape[1] // n_sub)) * n_sub
    gs2 = gs_ref.reshape(tm * GATE_ROWS, N_KEYS)
    w = []
    for j in range(n_sub):
        sl = slice(j * N_KEYS, (j + 1) * N_KEYS)
        gate = gs2[pl.ds(first + j, tm, stride=GATE_ROWS), :]
        w.append((gate * _gelu(hid[:, sl])).astype(BF16))
    w = jnp.concatenate(w, axis=-1)
    y_ref[...] += jnp.dot(w, v_ref[...], preferred_element_type=F32)


def peer_dense(xn, u_tab, v_tab, gates, h, tm, te):
    m, d = xn.shape
    n_exp = u_tab.shape[0]
    return pl.pallas_call(
        functools.partial(_peer_dense_kernel, tm=tm),
        grid=(m // tm, n_exp // te),
        in_specs=[pl.BlockSpec((tm, d), lambda i, e: (i, 0)),
                  pl.BlockSpec((te, d), lambda i, e: (e, 0)),
                  pl.BlockSpec((te, d), lambda i, e: (e, 0)),
                  pl.BlockSpec((tm, GATE_ROWS, N_KEYS), lambda i, e: (i, e // (GATE_ROWS * N_KEYS // te), 0)),
                  pl.BlockSpec(memory_space=pl.ANY)],
        out_specs=pl.BlockSpec((tm, d), lambda i, e: (i, 0)),
        out_shape=jax.ShapeDtypeStruct((m, d), F32),
        compiler_params=_params("parallel", "arbitrary"),
        name="peer_dense",
    )(xn, u_tab, v_tab, gates, h)


def _row_tile(m, cap):
    return min(m, cap)


def _trunk(x, attend, recur, w, lam_init):
    m, d = x.shape
    att_w = w["q_norm_cols"]
    xn = rmsnorm_bf16(x, w["norm1_w"], _row_tile(m, 256))
    z = matmul(xn, w["w_in"], tm=_row_tile(m, 1024), tn=512)
    qn, kn, knb, v, vb = qkv_post(z, w["q_norm_w"], w["k_norm_w"], att_w, _row_tile(m, 256))
    att = attend(qn, kn, knb, v, vb)
    rnn, s_fin = recur(z)
    mix = jnp.concatenate([att, rnn], axis=-1)
    h = matmul(mix, w["w_out"], x, tm=_row_tile(m, 1024), tn=512)
    hn = rmsnorm_bf16(h, w["norm2_w"], _row_tile(m, 256))
    pq = matmul(hn, w["peer_wq"], tm=_row_tile(m, 1024), tn=512)
    a, b, g = peer_topk(pq, w["peer_subkeys"], 128)
    gates = peer_gates(a.T, b.T, g.T, 64)
    y = peer_dense(hn, w["peer_u"], w["peer_v"], gates, h, _row_tile(m, 512), 512)
    return y, kn, v, s_fin


def kernel(x_prompt, x_sample, cache_k, cache_v, state_rnn, page_table, norm1_w, w_in, q_norm_w, k_norm_w, lambda_q1, lambda_k1, lambda_q2, lambda_k2, sub_norm_w, lb_param, rnn_norm_w, w_out, norm2_w, peer_wq, peer_subkeys, peer_u, peer_v):
    depth = w_in.shape[0]
    assert depth == 1, "single-layer trunk"
    layer = 0
    batch, seq, d = x_prompt.shape
    n_dec, dec_seq, _ = x_sample.shape
    assert dec_seq == 1
    h_att = cache_k.shape[3]
    att_w = h_att * VAL_DIM
    h_rnn = state_rnn.shape[2]
    lam_init = 0.8 - 0.6 * math.exp(-0.3 * layer)
    rnn_col0 = 3 * att_w // LANES

    w = dict(norm1_w=norm1_w[layer], w_in=w_in[layer], q_norm_w=q_norm_w[layer], k_norm_w=k_norm_w[layer],
             w_out=w_out[layer], norm2_w=norm2_w[layer], peer_wq=peer_wq[layer],
             peer_subkeys=peer_subkeys[layer], peer_u=peer_u[layer].astype(BF16),
             peer_v=peer_v[layer].astype(BF16), q_norm_cols=att_w)
    lams = [p[layer].reshape(1, HEAD_DIM) for p in (lambda_q1, lambda_k1, lambda_q2, lambda_k2)]
    sub_w = sub_norm_w[layer]
    rnn_w = rnn_norm_w[layer]

    s0 = jnp.zeros((batch, h_rnn, RNN_DK, RNN_DV), F32)
    y_p, k_p, v_p, s_p = _trunk(
        x_prompt.reshape(batch * seq, d),
        lambda qn, kn, knb, v, vb: attn_prompt(qn, knb, vb, lams, sub_w, batch, seq, lam_init, 256),
        lambda z: hgrn_prompt(z, lb_param, s0, rnn_w, batch, seq, rnn_col0),
        w, lam_init)

    m_pad = LANES
    xs = jnp.zeros((m_pad, d), F32).at[:n_dec].set(x_sample.reshape(n_dec, d))
    ck = cache_k[layer].reshape(cache_k.shape[1], PAGE_SIZE * h_att * 2, HEAD_DIM)
    cv = cache_v[layer]

    def attend_s(qn, kn, knb, v, vb):
        split = lambda a: a[:n_dec].astype(F32).reshape(n_dec, h_att, 2, HEAD_DIM).transpose(0, 2, 1, 3)
        o = attn_sample(split(qn), split(kn), v[:n_dec].reshape(n_dec, h_att, VAL_DIM), ck, cv, page_table,
                        lams, sub_w, lam_init, SAMPLE_PAGES_PER_STEP)
        return jnp.zeros((m_pad, att_w), BF16).at[:n_dec].set(o.reshape(n_dec, att_w))

    def recur_s(z):
        o, s_new = hgrn_sample(z, lb_param, state_rnn[layer], rnn_w, rnn_col0)
        return jnp.zeros((m_pad, o.shape[1]), BF16).at[:n_dec].set(o), s_new

    y_s, k_s, v_s, s_s = _trunk(xs, attend_s, recur_s, w, lam_init)

    return (y_p.reshape(batch, seq, d),
            y_s[:n_dec].reshape(n_dec, 1, d),
            k_p.reshape(1, batch, seq, h_att, 2, HEAD_DIM),
            v_p.reshape(1, batch, seq, h_att, VAL_DIM),
            k_s[:n_dec].reshape(1, n_dec, 1, h_att, 2, HEAD_DIM),
            v_s[:n_dec].reshape(1, n_dec, 1, h_att, VAL_DIM),
            s_p[None].astype(state_rnn.dtype),
            s_s[None].astype(state_rnn.dtype))
```

```python
import functools
import math

import jax
import jax.numpy as jnp
from jax import lax
from jax.experimental import pallas as pl
from jax.experimental.pallas import tpu as pltpu

F32 = jnp.float32
BF16 = jnp.bfloat16

LANES = 128
LOG2_LANES = 7
HEAD_DIM = 128
VAL_DIM = 2 * HEAD_DIM
RNN_DK = 128
RNN_DV = 128
HGRN_CHUNK = 16
NORM_EPS = 1e-6
ATTN_SCALE = HEAD_DIM ** -0.5
N_KEYS = 128
PEER_TOPK = 16
PAGE_SIZE = 128
GATE_ROWS = 8
SAMPLE_PAGES_PER_STEP = 4
VMEM_LIMIT = 56 * 1024 * 1024
PEER_DENSE_VMEM_LIMIT = 60 * 1024 * 1024

NT_DIMS = (((1,), (1,)), ((), ()))
TN_DIMS = (((0,), (0,)), ((), ()))


def _params(*sem):
    return pltpu.CompilerParams(dimension_semantics=sem, vmem_limit_bytes=VMEM_LIMIT)


def _sigmoid(x):
    return 1.0 / (1.0 + jnp.exp(-x))


def _rmsnorm_kernel(x_ref, w_ref, o_ref):
    x = x_ref[...]
    ms = jnp.mean(x * x, axis=-1, keepdims=True)
    o_ref[...] = (x * lax.rsqrt(ms + NORM_EPS) * w_ref[...]).astype(o_ref.dtype)


def rmsnorm_bf16(x, w, tm):
    m, d = x.shape
    return pl.pallas_call(
        _rmsnorm_kernel,
        grid=(m // tm,),
        in_specs=[pl.BlockSpec((tm, d), lambda i: (i, 0)),
                  pl.BlockSpec((1, d), lambda i: (0, 0))],
        out_specs=pl.BlockSpec((tm, d), lambda i: (i, 0)),
        out_shape=jax.ShapeDtypeStruct((m, d), BF16),
        compiler_params=_params("parallel"),
        name="rmsnorm",
    )(x, w.reshape(1, d))


def _matmul_kernel(*refs, has_res):
    if has_res:
        x_ref, w_ref, r_ref, o_ref, wb_ref = refs
    else:
        x_ref, w_ref, o_ref, wb_ref = refs

    @pl.when(pl.program_id(1) == 0)
    def _():
        wb_ref[...] = w_ref[...].astype(BF16)

    acc = jnp.dot(x_ref[...], wb_ref[...], preferred_element_type=F32)
    if has_res:
        acc = r_ref[...] + acc
    o_ref[...] = acc


def matmul(x, w, res=None, *, tm, tn):
    m, k = x.shape
    n = w.shape[1]
    in_specs = [pl.BlockSpec((tm, k), lambda j, i: (i, 0)),
                pl.BlockSpec((k, tn), lambda j, i: (0, j))]
    args = [x, w]
    if res is not None:
        in_specs.append(pl.BlockSpec((tm, tn), lambda j, i: (i, j)))
        args.append(res)
    return pl.pallas_call(
        functools.partial(_matmul_kernel, has_res=res is not None),
        grid=(n // tn, m // tm),
        in_specs=in_specs,
        out_specs=pl.BlockSpec((tm, tn), lambda j, i: (i, j)),
        out_shape=jax.ShapeDtypeStruct((m, n), F32),
        scratch_shapes=[pltpu.VMEM((k, tn), BF16)],
        compiler_params=_params("arbitrary", "arbitrary"),
        name="matmul",
    )(*args)


def _qkv_kernel(zq_ref, zk_ref, zv_ref, qw_ref, kw_ref, qn_ref, kn_ref, knb_ref, v_ref, vb_ref):
    width = zq_ref.shape[1]
    for g in range(width // HEAD_DIM):
        sl = slice(g * HEAD_DIM, (g + 1) * HEAD_DIM)
        xq = zq_ref[:, sl]
        yq = xq * lax.rsqrt(jnp.mean(xq * xq, axis=-1, keepdims=True) + NORM_EPS) * qw_ref[...]
        qn_ref[:, sl] = yq.astype(BF16)
        xk = zk_ref[:, sl]
        yk = xk * lax.rsqrt(jnp.mean(xk * xk, axis=-1, keepdims=True) + NORM_EPS) * kw_ref[...]
        kn_ref[:, sl] = yk
        knb_ref[:, sl] = yk.astype(BF16)
    v = zv_ref[...]
    v_ref[...] = v
    vb_ref[...] = v.astype(BF16)


def qkv_post(z, q_norm_w, k_norm_w, width, tm):
    m = z.shape[0]
    blk = lambda c: pl.BlockSpec((tm, width), lambda i, c=c: (i, c))
    wspec = pl.BlockSpec((1, HEAD_DIM), lambda i: (0, 0))
    out = lambda dt: jax.ShapeDtypeStruct((m, width), dt)
    return pl.pallas_call(
        _qkv_kernel,
        grid=(m // tm,),
        in_specs=[blk(0), blk(1), blk(2), wspec, wspec],
        out_specs=[blk(0)] * 5,
        out_shape=[out(BF16), out(F32), out(BF16), out(F32), out(BF16)],
        compiler_params=_params("parallel"),
        name="qkv_post",
    )(z, z, z, q_norm_w.reshape(1, HEAD_DIM), k_norm_w.reshape(1, HEAD_DIM))


def _lambda(lq1_ref, lk1_ref, lq2_ref, lk2_ref, lam_init):
    a = jnp.sum(lq1_ref[...] * lk1_ref[...], axis=-1, keepdims=True)
    b = jnp.sum(lq2_ref[...] * lk2_ref[...], axis=-1, keepdims=True)
    return jnp.exp(a) - jnp.exp(b) + lam_init


def _sub_norm(o, w, lam_init):
    ms = jnp.mean(o * o, axis=-1, keepdims=True)
    return o * lax.rsqrt(ms + NORM_EPS) * w * (1.0 - lam_init)


def _attn_prompt_kernel(q_ref, k_ref, v_ref, lq1_ref, lk1_ref, lq2_ref, lk2_ref, sw_ref, o_ref,
                        m_ref, l_ref, acc_ref, *, tq, lam_init):
    qi = pl.program_id(2)
    m_ref[...] = jnp.full(m_ref.shape, -jnp.inf, F32)
    l_ref[...] = jnp.zeros(l_ref.shape, F32)
    acc_ref[...] = jnp.zeros(acc_ref.shape, F32)
    tk = 2 * tq
    n_full = qi // 2

    def block(j, masked):
        kv_rows = pl.ds(pl.multiple_of(j * tk, tk), tk)
        vblk = v_ref[kv_rows, :]
        for c in range(2):
            sl = slice(c * HEAD_DIM, (c + 1) * HEAD_DIM)
            s = lax.dot_general(q_ref[:, sl], k_ref[kv_rows, sl], NT_DIMS,
                                preferred_element_type=F32) * ATTN_SCALE
            if masked:
                row = qi * tq + lax.broadcasted_iota(jnp.int32, (tq, tk), 0)
                col = j * tk + lax.broadcasted_iota(jnp.int32, (tq, tk), 1)
                s = jnp.where(col <= row, s, -jnp.inf)
            m_old = m_ref[c]
            m_new = jnp.maximum(m_old, jnp.max(s, axis=-1, keepdims=True))
            alpha = jnp.exp(m_old - m_new)
            p = jnp.exp(s - m_new)
            l_ref[c] = alpha * l_ref[c] + jnp.sum(p, axis=-1, keepdims=True)
            acc_ref[c] = alpha * acc_ref[c] + jnp.dot(p.astype(BF16), vblk, preferred_element_type=F32)
            m_ref[c] = m_new

    def full_block(j, carry):
        block(j, masked=False)
        return carry

    lax.fori_loop(0, n_full, full_block, 0)
    block(n_full, masked=True)
    lam = _lambda(lq1_ref, lk1_ref, lq2_ref, lk2_ref, lam_init)
    o = acc_ref[0] / l_ref[0] - lam * (acc_ref[1] / l_ref[1])
    o_ref[...] = _sub_norm(o, sw_ref[...], lam_init).astype(o_ref.dtype)


def attn_prompt(qn, knb, vb, lams, sub_norm_w, batch, seq, lam_init, tq):
    m, width = qn.shape
    heads = width // VAL_DIM
    nq = seq // tq
    vec = pl.BlockSpec((1, HEAD_DIM), lambda b, h, i: (0, 0))
    return pl.pallas_call(
        functools.partial(_attn_prompt_kernel, tq=tq, lam_init=lam_init),
        grid=(batch, heads, nq),
        in_specs=[pl.BlockSpec((tq, VAL_DIM), lambda b, h, i: (b * nq + i, h)),
                  pl.BlockSpec((seq, VAL_DIM), lambda b, h, i: (b, h)),
                  pl.BlockSpec((seq, VAL_DIM), lambda b, h, i: (b, h)),
                  vec, vec, vec, vec,
                  pl.BlockSpec((1, VAL_DIM), lambda b, h, i: (0, 0))],
        out_specs=pl.BlockSpec((tq, VAL_DIM), lambda b, h, i: (b * nq + i, h)),
        out_shape=jax.ShapeDtypeStruct((m, width), BF16),
        scratch_shapes=[pltpu.VMEM((2, tq, 1), F32), pltpu.VMEM((2, tq, 1), F32),
                        pltpu.VMEM((2, tq, VAL_DIM), F32)],
        compiler_params=_params("parallel", "parallel", "arbitrary"),
        name="attn_prompt",
    )(qn, knb, vb, *lams, sub_norm_w.reshape(1, VAL_DIM))


def _attn_sample_kernel(pt_ref, q_ref, kn_ref, vn_ref, lq1_ref, lk1_ref, lq2_ref, lk2_ref, sw_ref, *rest,
                        n_grp, lam_init):
    del pt_ref
    k_refs, v_refs = rest[:n_grp], rest[n_grp:2 * n_grp]
    o_ref, m_ref, l_ref, acc_ref = rest[2 * n_grp:]
    step = pl.program_id(1)
    heads = q_ref.shape[1]

    @pl.when(step == 0)
    def _():
        m_ref[...] = jnp.full(m_ref.shape, -jnp.inf, F32)
        l_ref[...] = jnp.zeros(l_ref.shape, F32)
        acc_ref[...] = jnp.zeros(acc_ref.shape, F32)

    q = [q_ref[c] * ATTN_SCALE for c in range(2)]
    rows = PAGE_SIZE * heads
    r_blk = jnp.right_shift(lax.broadcasted_iota(jnp.int32, (VAL_DIM, VAL_DIM), 0), LOG2_LANES)
    c_blk = jnp.right_shift(lax.broadcasted_iota(jnp.int32, (VAL_DIM, VAL_DIM), 1), LOG2_LANES)
    ones_blk = jnp.where(r_blk == c_blk, 1.0, 0.0).astype(BF16)
    scores = []
    for g in range(n_grp):
        prod = [(k_refs[g][pl.ds(c, rows, stride=2), :].reshape(PAGE_SIZE, heads, HEAD_DIM) * q[c][None])
                .reshape(rows, HEAD_DIM) for c in range(2)]
        lhs = jnp.concatenate(prod, axis=-1).astype(BF16)
        scores.append(jnp.dot(lhs, ones_blk, preferred_element_type=F32).reshape(PAGE_SIZE, heads, VAL_DIM))
    for c in range(2):
        half = slice(c * HEAD_DIM, (c + 1) * HEAD_DIM)
        m_old = m_ref[c]
        m_new = m_old
        for s in scores:
            m_new = jnp.maximum(m_new, jnp.max(s[:, :, half], axis=0))
        alpha = jnp.exp(m_old - m_new)
        l_new = alpha * l_ref[c]
        acc = jnp.concatenate([alpha, alpha], axis=-1) * acc_ref[c]
        for g in range(n_grp):
            p = jnp.exp(scores[g][:, :, half] - m_new[None])
            l_new = l_new + jnp.sum(p, axis=0)
            acc = acc + jnp.sum(jnp.concatenate([p, p], axis=-1) * v_refs[g][...], axis=0)
        m_ref[c] = m_new
        l_ref[c] = l_new
        acc_ref[c] = acc

    @pl.when(step == pl.num_programs(1) - 1)
    def _():
        outs = []
        for c in range(2):
            s_new = jnp.sum(kn_ref[c] * q[c], axis=-1, keepdims=True)
            m_old = m_ref[c]
            m_new = jnp.maximum(m_old, s_new)
            alpha = jnp.exp(m_old - m_new)
            p = jnp.exp(s_new - m_new)
            wide = lambda a: jnp.concatenate([a, a], axis=-1)
            outs.append((wide(alpha) * acc_ref[c] + wide(p) * vn_ref[...]) / wide(alpha * l_ref[c] + p))
        lam = _lambda(lq1_ref, lk1_ref, lq2_ref, lk2_ref, lam_init)
        o_ref[...] = _sub_norm(outs[0] - lam * outs[1], sw_ref[...], lam_init).astype(o_ref.dtype)


def attn_sample(q, kn, vn, cache_k, cache_v, page_table, lams, sub_norm_w, lam_init, n_grp):
    bsz, _, heads, _ = q.shape
    n_pages = page_table.shape[1]
    qspec = pl.BlockSpec((None, 2, heads, HEAD_DIM), lambda b, s, pt: (b, 0, 0, 0))
    vspec = pl.BlockSpec((None, heads, VAL_DIM), lambda b, s, pt: (b, 0, 0))
    vec = pl.BlockSpec((1, HEAD_DIM), lambda b, s, pt: (0, 0))

    def page(shape, g):
        zeros = (0,) * (len(shape) - 1)
        return pl.BlockSpec((None,) + tuple(shape[1:]),
                            lambda b, s, pt: (pt[b * n_pages + s * n_grp + g],) + zeros)

    grid_spec = pltpu.PrefetchScalarGridSpec(
        num_scalar_prefetch=1,
        grid=(bsz, n_pages // n_grp),
        in_specs=[qspec, qspec, vspec, vec, vec, vec, vec, pl.BlockSpec((1, VAL_DIM), lambda b, s, pt: (0, 0))]
        + [page(cache_k.shape, g) for g in range(n_grp)] + [page(cache_v.shape, g) for g in range(n_grp)],
        out_specs=vspec,
        scratch_shapes=[pltpu.VMEM((2, heads, HEAD_DIM), F32), pltpu.VMEM((2, heads, HEAD_DIM), F32),
                        pltpu.VMEM((2, heads, VAL_DIM), F32)],
    )
    return pl.pallas_call(
        functools.partial(_attn_sample_kernel, n_grp=n_grp, lam_init=lam_init),
        grid_spec=grid_spec,
        out_shape=jax.ShapeDtypeStruct((bsz, heads, VAL_DIM), BF16),
        compiler_params=_params("parallel", "arbitrary"),
        name="attn_sample",
    )(page_table.reshape(-1), q, kn, vn, *lams, sub_norm_w.reshape(1, VAL_DIM),
      *([cache_k] * n_grp), *([cache_v] * n_grp))


def _lower_bound(lb_ref):
    p = lb_ref[...]
    e = jnp.exp(p - jnp.max(p, axis=0, keepdims=True))
    return e[0:1, :] / jnp.sum(e, axis=0, keepdims=True)


def _split3(x):
    hi = x.astype(BF16)
    r = x - hi.astype(F32)
    mid = r.astype(BF16)
    lo = (r - mid.astype(F32)).astype(BF16)
    return hi, mid, lo


def _hgrn_prompt_kernel(zq_ref, zf_ref, zi_ref, zg_ref, lb_ref, s0_ref, nw_ref, o_ref, sfin_ref,
                        q_s, k_s, b_s, oi_s, st_s, dec_s, *, seq):
    c = HGRN_CHUNK
    n_chunks = seq // c
    lb = _lower_bound(lb_ref)
    zf = zf_ref[...]
    f = lb + (1.0 - lb) * _sigmoid(zf)
    k_s[...] = (1.0 - lb) * _sigmoid(-zf)
    zq = zq_ref[...]
    q_s[...] = zq * _sigmoid(zq)
    logf = jnp.log(f)

    grp = 256
    r = lax.broadcasted_iota(jnp.int32, (grp, grp), 0)
    cc = lax.broadcasted_iota(jnp.int32, (grp, grp), 1)
    shift = c.bit_length() - 1
    same_chunk = jnp.right_shift(r, shift) == jnp.right_shift(cc, shift)
    tri = jnp.where(same_chunk & (cc <= r), 1.0, 0.0).astype(BF16)
    b_s[...] = logf
    for gi in range(seq // grp):
        rows = slice(gi * grp, (gi + 1) * grp)
        hi, mid, lo = _split3(b_s[rows, :])
        b_s[rows, :] = (jnp.dot(tri, hi, preferred_element_type=F32)
                        + jnp.dot(tri, mid, preferred_element_type=F32)
                        + jnp.dot(tri, lo, preferred_element_type=F32))

    half = c // 2
    hrow = lax.broadcasted_iota(jnp.int32, (half, RNN_DK), 0)

    def phase1(n, carry):
        rows = pl.ds(pl.multiple_of(n * c, c), c)
        b = b_s[rows, :]
        q = q_s[rows, :]
        k = k_s[rows, :]
        v = zi_ref[rows, :]
        b_last = b[c - 1:c, :]
        b_h, q_h = (b[:half], b[half:]), (q[:half], q[half:])
        o_h = [jnp.zeros((half, RNN_DV), F32), jnp.zeros((half, RNN_DV), F32)]
        for s in range(c):
            bs, ks, vs = b[s:s + 1, :], k[s:s + 1, :], v[s:s + 1, :]
            own = s // half
            keep = hrow >= s - own * half
            pw = jnp.where(keep, q_h[own] * ks * jnp.exp(jnp.where(keep, b_h[own] - bs, 0.0)), 0.0)
            o_h[own] = o_h[own] + jnp.sum(pw, axis=-1, keepdims=True) * vs
            if own == 0:
                pw = q_h[1] * ks * jnp.exp(b_h[1] - bs)
                o_h[1] = o_h[1] + jnp.sum(pw, axis=-1, keepdims=True) * vs
        oi_s[rows, :] = jnp.concatenate(o_h, axis=0)
        q_s[rows, :] = q * jnp.exp(b)
        kt = k * jnp.exp(b_last - b)
        st_s[n] = lax.dot_general(v.astype(BF16), kt.astype(BF16), TN_DIMS, preferred_element_type=F32)
        dec_s[pl.ds(n, 1), :] = jnp.exp(b_last)
        return carry

    lax.fori_loop(0, n_chunks, phase1, 0, unroll=4)

    def phase2(n, st):
        inc = st_s[n]
        st_s[n] = st
        return st * dec_s[pl.ds(n, 1), :] + inc

    st_fin = lax.fori_loop(0, n_chunks, phase2, s0_ref[...].T)
    sfin_ref[...] = st_fin.T

    def phase3(n, carry):
        rows = pl.ds(pl.multiple_of(n * c, c), c)
        oi_s[rows, :] += lax.dot_general(q_s[rows, :].astype(BF16), st_s[n].astype(BF16), NT_DIMS,
                                         preferred_element_type=F32)
        return carry

    lax.fori_loop(0, n_chunks, phase3, 0, unroll=8)
    o = oi_s[...]
    ms = jnp.mean(o * o, axis=-1, keepdims=True)
    zg = zg_ref[...]
    o_ref[...] = (o * lax.rsqrt(ms + NORM_EPS) * nw_ref[...] * (zg * _sigmoid(zg))).astype(o_ref.dtype)


def hgrn_prompt(z, lb_param, s0, rnn_norm_w, batch, seq, col0):
    m = z.shape[0]
    heads = s0.shape[1]
    zblk = lambda g: pl.BlockSpec((seq, RNN_DK), lambda b, h, g=g: (b, col0 + g * heads + h))
    n_chunks = seq // HGRN_CHUNK
    return pl.pallas_call(
        functools.partial(_hgrn_prompt_kernel, seq=seq),
        grid=(batch, heads),
        in_specs=[zblk(0), zblk(1), zblk(2), zblk(3),
                  pl.BlockSpec((lb_param.shape[0], RNN_DK), lambda b, h: (0, h)),
                  pl.BlockSpec((None, None, RNN_DK, RNN_DV), lambda b, h: (b, h, 0, 0)),
                  pl.BlockSpec((1, RNN_DV), lambda b, h: (0, 0))],
        out_specs=[pl.BlockSpec((seq, RNN_DV), lambda b, h: (b, h)),
                   pl.BlockSpec((None, None, RNN_DK, RNN_DV), lambda b, h: (b, h, 0, 0))],
        out_shape=[jax.ShapeDtypeStruct((m, heads * RNN_DV), BF16),
                   jax.ShapeDtypeStruct(s0.shape, F32)],
        scratch_shapes=[pltpu.VMEM((seq, RNN_DK), F32), pltpu.VMEM((seq, RNN_DK), F32),
                        pltpu.VMEM((seq, RNN_DK), F32), pltpu.VMEM((seq, RNN_DV), F32),
                        pltpu.VMEM((n_chunks, RNN_DV, RNN_DK), F32),
                        pltpu.VMEM((n_chunks, RNN_DK), F32)],
        compiler_params=_params("parallel", "parallel"),
        name="hgrn_prompt",
    )(z, z, z, z, lb_param, s0, rnn_norm_w.reshape(1, RNN_DV))


def _hgrn_sample_kernel(zq_ref, zf_ref, zi_ref, zg_ref, lb_ref, s0_ref, nw_ref, o_ref, s_ref):
    bsz = s0_ref.shape[0]
    lb = _lower_bound(lb_ref)
    zf = zf_ref[...]
    f_t = (lb + (1.0 - lb) * _sigmoid(zf)).T
    k_t = ((1.0 - lb) * _sigmoid(-zf)).T
    zq = zq_ref[...]
    q_t = (zq * _sigmoid(zq)).T
    for b in range(bsz):
        v = zi_ref[b:b + 1, :]
        s_new = f_t[:, b:b + 1] * s0_ref[b] + k_t[:, b:b + 1] * v
        s_ref[b] = s_new
        o = jnp.sum(q_t[:, b:b + 1] * s_new, axis=0, keepdims=True)
        ms = jnp.mean(o * o, axis=-1, keepdims=True)
        zg = zg_ref[b:b + 1, :]
        o_ref[b:b + 1, :] = (o * lax.rsqrt(ms + NORM_EPS) * nw_ref[...] * (zg * _sigmoid(zg))).astype(o_ref.dtype)


def hgrn_sample(z, lb_param, s0, rnn_norm_w, col0):
    bsz, heads = s0.shape[:2]
    zblk = lambda g: pl.BlockSpec((bsz, RNN_DK), lambda h, g=g: (0, col0 + g * heads + h))
    return pl.pallas_call(
        _hgrn_sample_kernel,
        grid=(heads,),
        in_specs=[zblk(0), zblk(1), zblk(2), zblk(3),
                  pl.BlockSpec((lb_param.shape[0], RNN_DK), lambda h: (0, h)),
                  pl.BlockSpec((bsz, None, RNN_DK, RNN_DV), lambda h: (0, h, 0, 0)),
                  pl.BlockSpec((1, RNN_DV), lambda h: (0, 0))],
        out_specs=[pl.BlockSpec((bsz, RNN_DV), lambda h: (0, h)),
                   pl.BlockSpec((bsz, None, RNN_DK, RNN_DV), lambda h: (0, h, 0, 0))],
        out_shape=[jax.ShapeDtypeStruct((bsz, heads * RNN_DV), BF16),
                   jax.ShapeDtypeStruct(s0.shape, F32)],
        compiler_params=_params("parallel"),
        name="hgrn_sample",
    )(z, z, z, z, lb_param, s0, rnn_norm_w.reshape(1, RNN_DV))


def _top16(s, ids):
    big = jnp.int32(2 ** 30)
    vals, sel = [], []
    for _ in range(PEER_TOPK):
        m = jnp.max(s, axis=0, keepdims=True)
        pick = jnp.min(jnp.where(s == m, ids, big), axis=0, keepdims=True)
        s = jnp.where(ids == pick, -jnp.inf, s)
        vals.append(m)
        sel.append(pick)
    return jnp.concatenate(vals, axis=0), jnp.concatenate(sel, axis=0)


def _peer_topk_kernel(q_ref, sk_ref, a_ref, b_ref, g_ref):
    tm = q_ref.shape[0]
    heads = sk_ref.shape[0]
    key_ids = lax.broadcasted_iota(jnp.int32, (N_KEYS, tm), 0)
    k = PEER_TOPK
    for h in range(heads):
        half = []
        for c in range(2):
            col = (2 * h + c) * HEAD_DIM
            sc = lax.dot_general(sk_ref[h, c].astype(BF16), q_ref[:, col:col + HEAD_DIM].astype(BF16),
                                 NT_DIMS, preferred_element_type=F32)
            half.append(_top16(sc, key_ids))
        (s1, i1), (s2, i2) = half
        cand, eid, fid = [], [], []
        crow = lax.broadcasted_iota(jnp.int32, (k, tm), 0)
        for r in range(k // 2):
            nc = k if r == 0 else k // 2
            cand.append(s1[r:r + 1, :] + s2[:nc, :])
            eid.append(i1[r:r + 1, :] * N_KEYS + i2[:nc, :])
            fid.append(r * k + crow[:nc, :])
        cand.append(s1[k // 2:, :] + s2[0:1, :])
        eid.append(i1[k // 2:, :] * N_KEYS + i2[0:1, :])
        fid.append((k // 2 + crow[:k // 2, :]) * k)
        cand = jnp.concatenate(cand, axis=0)
        eid = jnp.concatenate(eid, axis=0)
        fid = jnp.concatenate(fid, axis=0)
        best, pick = _top16(cand, fid)
        e_sel = []
        for j in range(k):
            e_sel.append(jnp.max(jnp.where(fid == pick[j:j + 1, :], eid, -1), axis=0, keepdims=True))
        e_sel = jnp.concatenate(e_sel, axis=0)
        p = jnp.exp(best - best[0:1, :])
        gate = p / jnp.sum(p, axis=0, keepdims=True)
        rows = slice(h * k, (h + 1) * k)
        a_ref[rows, :] = jnp.right_shift(e_sel, LOG2_LANES)
        b_ref[rows, :] = jnp.bitwise_and(e_sel, N_KEYS - 1)
        g_ref[rows, :] = gate


def peer_topk(q, subkeys, tm):
    m = q.shape[0]
    heads = subkeys.shape[0]
    n_sel = heads * PEER_TOPK
    out = pl.BlockSpec((n_sel, tm), lambda i: (0, i))
    return pl.pallas_call(
        _peer_topk_kernel,
        grid=(m // tm,),
        in_specs=[pl.BlockSpec((tm, q.shape[1]), lambda i: (i, 0)),
                  pl.BlockSpec(subkeys.shape, lambda i: (0, 0, 0, 0))],
        out_specs=[out, out, out],
        out_shape=[jax.ShapeDtypeStruct((n_sel, m), jnp.int32),
                   jax.ShapeDtypeStruct((n_sel, m), jnp.int32),
                   jax.ShapeDtypeStruct((n_sel, m), F32)],
        compiler_params=_params("parallel"),
        name="peer_topk",
    )(q, subkeys)


def _peer_gates_kernel(a_ref, b_ref, g_ref, o_ref):
    tg, n_sel = a_ref.shape
    ids = lax.broadcasted_iota(jnp.int32, (N_KEYS, n_sel), 0)

    def body(t, carry):
        row = pl.ds(t, 1)
        left = jnp.where(ids == a_ref[row, :], 1.0, 0.0).astype(BF16)
        right = jnp.where(ids == b_ref[row, :], g_ref[row, :], 0.0).astype(BF16)
        o_ref[t] = lax.dot_general(left, right, NT_DIMS, preferred_element_type=F32).astype(o_ref.dtype)
        return carry

    lax.fori_loop(0, tg, body, 0, unroll=8)


def peer_gates(a, b, g, tg):
    m, n_sel = a.shape
    spec = pl.BlockSpec((tg, n_sel), lambda i: (i, 0))
    return pl.pallas_call(
        _peer_gates_kernel,
        grid=(m // tg,),
        in_specs=[spec, spec, spec],
        out_specs=pl.BlockSpec((tg, N_KEYS, N_KEYS), lambda i: (i, 0, 0)),
        out_shape=jax.ShapeDtypeStruct((m, N_KEYS, N_KEYS), F32),
        compiler_params=_params("parallel"),
        name="peer_gates",
    )(a, b, g)


def _gelu(x):
    return 0.5 * x * (1.0 + lax.erf(x * (2.0 ** -0.5)))


def _peer_dense_kernel(x_ref, u_ref, v_ref, gs_ref, h_hbm, y_hbm, y_ref, *, tm):
    tile = pl.ds(pl.program_id(0) * tm, tm)

    @pl.when(pl.program_id(1) == 0)
    def _():
        pltpu.sync_copy(h_hbm.at[tile, :], y_ref)

    hid = lax.dot_general(x_ref[...], u_ref[...], NT_DIMS, preferred_element_type=F32)
    n_sub = hid.shape[1] // N_KEYS
    first = (pl.program_id(1) % (gs_ref.shape[1] // n_sub)) * n_sub
    gs2 = gs_ref.reshape(tm * GATE_ROWS, N_KEYS)
    w = []
    for j in range(n_sub):
        sl = slice(j * N_KEYS, (j + 1) * N_KEYS)
        gate = gs2[pl.ds(first + j, tm, stride=GATE_ROWS), :]
        w.append((gate * _gelu(hid[:, sl])).astype(BF16))
    w = jnp.concatenate(w, axis=-1)
    y_ref[...] += jnp.dot(w, v_ref[...], preferred_element_type=F32)

    @pl.when(pl.program_id(1) == pl.num_programs(1) - 1)
    def _():
        pltpu.sync_copy(y_ref, y_hbm.at[tile, :])


def peer_dense(xn, u_tab, v_tab, gates, h, tm, te):
    m, d = xn.shape
    n_exp = u_tab.shape[0]
    return pl.pallas_call(
        functools.partial(_peer_dense_kernel, tm=tm),
        grid=(m // tm, n_exp // te),
        in_specs=[pl.BlockSpec((tm, d), lambda i, e: (i, 0), pipeline_mode=pl.Buffered(1)),
                  pl.BlockSpec((te, d), lambda i, e: (e, 0)),
                  pl.BlockSpec((te, d), lambda i, e: (e, 0)),
                  pl.BlockSpec((tm, GATE_ROWS, N_KEYS), lambda i, e: (i, e // (GATE_ROWS * N_KEYS // te), 0)),
                  pl.BlockSpec(memory_space=pl.ANY)],
        out_specs=pl.BlockSpec(memory_space=pl.ANY),
        out_shape=jax.ShapeDtypeStruct((m, d), F32),
        scratch_shapes=[pltpu.VMEM((tm, d), F32)],
        compiler_params=pltpu.CompilerParams(dimension_semantics=("parallel", "arbitrary"),
                                             vmem_limit_bytes=PEER_DENSE_VMEM_LIMIT),
        name="peer_dense",
    )(xn, u_tab, v_tab, gates, h)


def _row_tile(m, cap):
    return min(m, cap)


def _trunk(x, attend, recur, w, lam_init):
    m, d = x.shape
    att_w = w["q_norm_cols"]
    xn = rmsnorm_bf16(x, w["norm1_w"], _row_tile(m, 256))
    z = matmul(xn, w["w_in"], tm=_row_tile(m, 1024), tn=512)
    qn, kn, knb, v, vb = qkv_post(z, w["q_norm_w"], w["k_norm_w"], att_w, _row_tile(m, 256))
    att = attend(qn, kn, knb, v, vb)
    rnn, s_fin = recur(z)
    mix = jnp.concatenate([att, rnn], axis=-1)
    h = matmul(mix, w["w_out"], x, tm=_row_tile(m, 1024), tn=512)
    hn = rmsnorm_bf16(h, w["norm2_w"], _row_tile(m, 256))
    pq = matmul(hn, w["peer_wq"], tm=_row_tile(m, 1024), tn=512)
    a, b, g = peer_topk(pq, w["peer_subkeys"], 128)
    gates = peer_gates(a.T, b.T, g.T, 64)
    y = peer_dense(hn, w["peer_u"], w["peer_v"], gates, h, _row_tile(m, 1024), 512)
    return y, kn, v, s_fin


def kernel(x_prompt, x_sample, cache_k, cache_v, state_rnn, page_table, norm1_w, w_in, q_norm_w, k_norm_w, lambda_q1, lambda_k1, lambda_q2, lambda_k2, sub_norm_w, lb_param, rnn_norm_w, w_out, norm2_w, peer_wq, peer_subkeys, peer_u, peer_v):
    depth = w_in.shape[0]
    assert depth == 1, "single-layer trunk"
    layer = 0
    batch, seq, d = x_prompt.shape
    n_dec, dec_seq, _ = x_sample.shape
    assert dec_seq == 1
    h_att = cache_k.shape[3]
    att_w = h_att * VAL_DIM
    h_rnn = state_rnn.shape[2]
    lam_init = 0.8 - 0.6 * math.exp(-0.3 * layer)
    rnn_col0 = 3 * att_w // LANES

    w = dict(norm1_w=norm1_w[layer], w_in=w_in[layer], q_norm_w=q_norm_w[layer], k_norm_w=k_norm_w[layer],
             w_out=w_out[layer], norm2_w=norm2_w[layer], peer_wq=peer_wq[layer],
             peer_subkeys=peer_subkeys[layer], peer_u=peer_u[layer].astype(BF16),
             peer_v=peer_v[layer].astype(BF16), q_norm_cols=att_w)
    lams = [p[layer].reshape(1, HEAD_DIM) for p in (lambda_q1, lambda_k1, lambda_q2, lambda_k2)]
    sub_w = sub_norm_w[layer]
    rnn_w = rnn_norm_w[layer]

    s0 = jnp.zeros((batch, h_rnn, RNN_DK, RNN_DV), F32)
    y_p, k_p, v_p, s_p = _trunk(
        x_prompt.reshape(batch * seq, d),
        lambda qn, kn, knb, v, vb: attn_prompt(qn, knb, vb, lams, sub_w, batch, seq, lam_init, 256),
        lambda z: hgrn_prompt(z, lb_param, s0, rnn_w, batch, seq, rnn_col0),
        w, lam_init)

    m_pad = LANES
    xs = jnp.zeros((m_pad, d), F32).at[:n_dec].set(x_sample.reshape(n_dec, d))
    ck = cache_k[layer].reshape(cache_k.shape[1], PAGE_SIZE * h_att * 2, HEAD_DIM)
    cv = cache_v[layer]

    def attend_s(qn, kn, knb, v, vb):
        split = lambda a: a[:n_dec].astype(F32).reshape(n_dec, h_att, 2, HEAD_DIM).transpose(0, 2, 1, 3)
        o = attn_sample(split(qn), split(kn), v[:n_dec].reshape(n_dec, h_att, VAL_DIM), ck, cv, page_table,
                        lams, sub_w, lam_init, SAMPLE_PAGES_PER_STEP)
        return jnp.zeros((m_pad, att_w), BF16).at[:n_dec].set(o.reshape(n_dec, att_w))

    def recur_s(z):
        o, s_new = hgrn_sample(z, lb_param, state_rnn[layer], rnn_w, rnn_col0)
        return jnp.zeros((m_pad, o.shape[1]), BF16).at[:n_dec].set(o), s_new

    y_s, k_s, v_s, s_s = _trunk(xs, attend_s, recur_s, w, lam_init)

    return (y_p.reshape(batch, seq, d),
            y_s[:n_dec].reshape(n_dec, 1, d),
            k_p.reshape(1, batch, seq, h_att, 2, HEAD_DIM),
            v_p.reshape(1, batch, seq, h_att, VAL_DIM),
            k_s[:n_dec].reshape(1, n_dec, 1, h_att, 2, HEAD_DIM),
            v_s[:n_dec].reshape(1, n_dec, 1, h_att, VAL_DIM),
            s_p[None].astype(state_rnn.dtype),
            s_s[None].astype(state_rnn.dtype))
```

```python
import functools
import math

import jax
import jax.numpy as jnp
from jax import lax
from jax.experimental import pallas as pl
from jax.experimental.pallas import tpu as pltpu

F32 = jnp.float32
BF16 = jnp.bfloat16

LANES = 128
LOG2_LANES = 7
HEAD_DIM = 128
VAL_DIM = 2 * HEAD_DIM
RNN_DK = 128
RNN_DV = 128
HGRN_CHUNK = 16
NORM_EPS = 1e-6
ATTN_SCALE = HEAD_DIM ** -0.5
LOG2_E = math.log2(math.e)
N_KEYS = 128
PEER_TOPK = 16
PAGE_SIZE = 128
GATE_ROWS = 8
SAMPLE_PAGES_PER_STEP = 8
ATTN_HEADS_PER_STEP = 2
ATTN_Q_BLOCK = 256
ATTN_KV_BLOCK = 512
VMEM_LIMIT = 56 * 1024 * 1024
PEER_DENSE_VMEM_LIMIT = 60 * 1024 * 1024

NT_DIMS = (((1,), (1,)), ((), ()))
TN_DIMS = (((0,), (0,)), ((), ()))


def _params(*sem):
    return pltpu.CompilerParams(dimension_semantics=sem, vmem_limit_bytes=VMEM_LIMIT)


def _sigmoid(x):
    return 1.0 / (1.0 + jnp.exp(-x))


def _rmsnorm_kernel(x_ref, w_ref, o_ref):
    x = x_ref[...]
    ms = jnp.mean(x * x, axis=-1, keepdims=True)
    o_ref[...] = (x * lax.rsqrt(ms + NORM_EPS) * w_ref[...]).astype(o_ref.dtype)


def rmsnorm_bf16(x, w, tm):
    m, d = x.shape
    return pl.pallas_call(
        _rmsnorm_kernel,
        grid=(m // tm,),
        in_specs=[pl.BlockSpec((tm, d), lambda i: (i, 0)),
                  pl.BlockSpec((1, d), lambda i: (0, 0))],
        out_specs=pl.BlockSpec((tm, d), lambda i: (i, 0)),
        out_shape=jax.ShapeDtypeStruct((m, d), BF16),
        compiler_params=_params("parallel"),
        name="rmsnorm",
    )(x, w.reshape(1, d))


def _matmul_kernel(*refs, has_res):
    if has_res:
        x_ref, w_ref, r_ref, o_ref, wb_ref = refs
    else:
        x_ref, w_ref, o_ref, wb_ref = refs

    @pl.when(pl.program_id(1) == 0)
    def _():
        wb_ref[...] = w_ref[...].astype(BF16)

    acc = jnp.dot(x_ref[...], wb_ref[...], preferred_element_type=F32)
    if has_res:
        acc = r_ref[...] + acc
    o_ref[...] = acc


def matmul(x, w, res=None, *, tm, tn):
    m, k = x.shape
    n = w.shape[1]
    in_specs = [pl.BlockSpec((tm, k), lambda j, i: (i, 0)),
                pl.BlockSpec((k, tn), lambda j, i: (0, j))]
    args = [x, w]
    if res is not None:
        in_specs.append(pl.BlockSpec((tm, tn), lambda j, i: (i, j)))
        args.append(res)
    return pl.pallas_call(
        functools.partial(_matmul_kernel, has_res=res is not None),
        grid=(n // tn, m // tm),
        in_specs=in_specs,
        out_specs=pl.BlockSpec((tm, tn), lambda j, i: (i, j)),
        out_shape=jax.ShapeDtypeStruct((m, n), F32),
        scratch_shapes=[pltpu.VMEM((k, tn), BF16)],
        compiler_params=_params("arbitrary", "arbitrary"),
        name="matmul",
    )(*args)


def _qkv_kernel(zq_ref, zk_ref, zv_ref, qw_ref, kw_ref, qn_ref, kn_ref, knb_ref, v_ref, vb_ref):
    width = zq_ref.shape[1]
    for g in range(width // HEAD_DIM):
        sl = slice(g * HEAD_DIM, (g + 1) * HEAD_DIM)
        xq = zq_ref[:, sl]
        yq = xq * lax.rsqrt(jnp.mean(xq * xq, axis=-1, keepdims=True) + NORM_EPS) * qw_ref[...]
        qn_ref[:, sl] = yq.astype(BF16)
        xk = zk_ref[:, sl]
        yk = xk * lax.rsqrt(jnp.mean(xk * xk, axis=-1, keepdims=True) + NORM_EPS) * kw_ref[...]
        kn_ref[:, sl] = yk
        knb_ref[:, sl] = yk.astype(BF16)
    v = zv_ref[...]
    v_ref[...] = v
    vb_ref[...] = v.astype(BF16)


def qkv_post(z, q_norm_w, k_norm_w, width, tm):
    m = z.shape[0]
    blk = lambda c: pl.BlockSpec((tm, width), lambda i, c=c: (i, c))
    wspec = pl.BlockSpec((1, HEAD_DIM), lambda i: (0, 0))
    out = lambda dt: jax.ShapeDtypeStruct((m, width), dt)
    return pl.pallas_call(
        _qkv_kernel,
        grid=(m // tm,),
        in_specs=[blk(0), blk(1), blk(2), wspec, wspec],
        out_specs=[blk(0)] * 5,
        out_shape=[out(BF16), out(F32), out(BF16), out(F32), out(BF16)],
        compiler_params=_params("parallel"),
        name="qkv_post",
    )(z, z, z, q_norm_w.reshape(1, HEAD_DIM), k_norm_w.reshape(1, HEAD_DIM))


def _lambda(lq1_ref, lk1_ref, lq2_ref, lk2_ref, lam_init):
    a = jnp.sum(lq1_ref[...] * lk1_ref[...], axis=-1, keepdims=True)
    b = jnp.sum(lq2_ref[...] * lk2_ref[...], axis=-1, keepdims=True)
    return jnp.exp(a) - jnp.exp(b) + lam_init


def _sub_norm(o, w, lam_init):
    ms = jnp.mean(o * o, axis=-1, keepdims=True)
    return o * lax.rsqrt(ms + NORM_EPS) * w * (1.0 - lam_init)


def _attn_prompt_kernel(q_ref, k_ref, v_ref, lq1_ref, lk1_ref, lq2_ref, lk2_ref, sw_ref, o_ref,
                        m_ref, l_ref, acc_ref, *, tq, lam_init):
    qi = pl.program_id(2)
    m_ref[...] = jnp.full(m_ref.shape, -jnp.inf, F32)
    l_ref[...] = jnp.zeros(l_ref.shape, F32)
    acc_ref[...] = jnp.zeros(acc_ref.shape, F32)
    tk = ATTN_KV_BLOCK
    n_full = (qi * tq + 1) // tk

    n_chain = m_ref.shape[0]

    def block(j, masked):
        kv_rows = pl.ds(pl.multiple_of(j * tk, tk), tk)
        for ch in range(n_chain):
            sl = slice(ch * HEAD_DIM, (ch + 1) * HEAD_DIM)
            vsl = slice(ch // 2 * VAL_DIM, (ch // 2 + 1) * VAL_DIM)
            s = lax.dot_general(q_ref[:, sl], k_ref[kv_rows, sl], NT_DIMS,
                                preferred_element_type=F32) * (ATTN_SCALE * LOG2_E)
            if masked:
                row = qi * tq + lax.broadcasted_iota(jnp.int32, (tq, tk), 0)
                col = j * tk + lax.broadcasted_iota(jnp.int32, (tq, tk), 1)
                s = jnp.where(col <= row, s, -jnp.inf)
            m_old = m_ref[ch]
            m_new = jnp.maximum(m_old, jnp.max(s, axis=-1, keepdims=True))
            alpha = jnp.exp2(m_old - m_new)
            p = jnp.exp2(s - m_new)
            p_lanes = sum(p[:, t * LANES:(t + 1) * LANES] for t in range(1, tk // LANES)) + p[:, :LANES]
            l_ref[ch] = alpha * l_ref[ch] + p_lanes
            acc_ref[ch] = alpha * acc_ref[ch] + jnp.dot(p.astype(BF16), v_ref[kv_rows, vsl],
                                                        preferred_element_type=F32)
            m_ref[ch] = m_new

    def full_block(j, carry):
        block(j, masked=False)
        return carry

    lax.fori_loop(0, n_full, full_block, 0)
    block(n_full, masked=True)
    lam = _lambda(lq1_ref, lk1_ref, lq2_ref, lk2_ref, lam_init)
    for hh in range(n_chain // 2):
        l0 = jnp.sum(l_ref[2 * hh], axis=-1, keepdims=True)
        l1 = jnp.sum(l_ref[2 * hh + 1], axis=-1, keepdims=True)
        o = acc_ref[2 * hh] / l0 - lam * (acc_ref[2 * hh + 1] / l1)
        o_ref[:, hh * VAL_DIM:(hh + 1) * VAL_DIM] = _sub_norm(o, sw_ref[...], lam_init).astype(o_ref.dtype)


def attn_prompt(qn, knb, vb, lams, sub_norm_w, batch, seq, lam_init, tq):
    m, width = qn.shape
    hp = ATTN_HEADS_PER_STEP
    blk_w = hp * VAL_DIM
    nq = seq // tq
    vec = pl.BlockSpec((1, HEAD_DIM), lambda b, h, i: (0, 0))
    return pl.pallas_call(
        functools.partial(_attn_prompt_kernel, tq=tq, lam_init=lam_init),
        grid=(batch, width // blk_w, nq),
        in_specs=[pl.BlockSpec((tq, blk_w), lambda b, h, i: (b * nq + i, h)),
                  pl.BlockSpec((seq, blk_w), lambda b, h, i: (b, h)),
                  pl.BlockSpec((seq, blk_w), lambda b, h, i: (b, h)),
                  vec, vec, vec, vec,
                  pl.BlockSpec((1, VAL_DIM), lambda b, h, i: (0, 0))],
        out_specs=pl.BlockSpec((tq, blk_w), lambda b, h, i: (b * nq + i, h)),
        out_shape=jax.ShapeDtypeStruct((m, width), BF16),
        scratch_shapes=[pltpu.VMEM((2 * hp, tq, 1), F32), pltpu.VMEM((2 * hp, tq, LANES), F32),
                        pltpu.VMEM((2 * hp, tq, VAL_DIM), F32)],
        compiler_params=_params("parallel", "parallel", "arbitrary"),
        name="attn_prompt",
    )(qn, knb, vb, *lams, sub_norm_w.reshape(1, VAL_DIM))


def _attn_sample_kernel(pt_ref, q_ref, kn_ref, vn_ref, lq1_ref, lk1_ref, lq2_ref, lk2_ref, sw_ref, *rest,
                        n_grp, lam_init):
    del pt_ref
    k_refs, v_refs = rest[:n_grp], rest[n_grp:2 * n_grp]
    o_ref, m_ref, l_ref, acc_ref = rest[2 * n_grp:]
    step = pl.program_id(1)
    heads = q_ref.shape[1]

    @pl.when(step == 0)
    def _():
        m_ref[...] = jnp.full(m_ref.shape, -jnp.inf, F32)
        l_ref[...] = jnp.zeros(l_ref.shape, F32)
        acc_ref[...] = jnp.zeros(acc_ref.shape, F32)

    q = [q_ref[c] * (ATTN_SCALE * LOG2_E) for c in range(2)]
    rows = PAGE_SIZE * heads
    r_blk = jnp.right_shift(lax.broadcasted_iota(jnp.int32, (VAL_DIM, VAL_DIM), 0), LOG2_LANES)
    c_blk = jnp.right_shift(lax.broadcasted_iota(jnp.int32, (VAL_DIM, VAL_DIM), 1), LOG2_LANES)
    ones_blk = jnp.where(r_blk == c_blk, 1.0, 0.0).astype(BF16)
    scores = []
    for g in range(n_grp):
        prod = [(k_refs[g][pl.ds(c, rows, stride=2), :].reshape(PAGE_SIZE, heads, HEAD_DIM) * q[c][None])
                .reshape(rows, HEAD_DIM) for c in range(2)]
        lhs = jnp.concatenate(prod, axis=-1).astype(BF16)
        scores.append(jnp.dot(lhs, ones_blk, preferred_element_type=F32).reshape(PAGE_SIZE, heads, VAL_DIM))
    for c in range(2):
        half = slice(c * HEAD_DIM, (c + 1) * HEAD_DIM)
        m_old = m_ref[c]
        m_new = m_old
        for s in scores:
            m_new = jnp.maximum(m_new, jnp.max(s[:, :, half], axis=0))
        alpha = jnp.exp2(m_old - m_new)
        l_new = alpha * l_ref[c]
        acc = jnp.concatenate([alpha, alpha], axis=-1) * acc_ref[c]
        for g in range(n_grp):
            p = jnp.exp2(scores[g][:, :, half] - m_new[None])
            l_new = l_new + jnp.sum(p, axis=0)
            acc = acc + jnp.sum(jnp.concatenate([p, p], axis=-1) * v_refs[g][...], axis=0)
        m_ref[c] = m_new
        l_ref[c] = l_new
        acc_ref[c] = acc

    @pl.when(step == pl.num_programs(1) - 1)
    def _():
        outs = []
        for c in range(2):
            s_new = jnp.sum(kn_ref[c] * q[c], axis=-1, keepdims=True)
            m_old = m_ref[c]
            m_new = jnp.maximum(m_old, s_new)
            alpha = jnp.exp2(m_old - m_new)
            p = jnp.exp2(s_new - m_new)
            wide = lambda a: jnp.concatenate([a, a], axis=-1)
            outs.append((wide(alpha) * acc_ref[c] + wide(p) * vn_ref[...]) / wide(alpha * l_ref[c] + p))
        lam = _lambda(lq1_ref, lk1_ref, lq2_ref, lk2_ref, lam_init)
        o_ref[...] = _sub_norm(outs[0] - lam * outs[1], sw_ref[...], lam_init).astype(o_ref.dtype)


def attn_sample(q, kn, vn, cache_k, cache_v, page_table, lams, sub_norm_w, lam_init, n_grp):
    bsz, _, heads, _ = q.shape
    n_pages = page_table.shape[1]
    qspec = pl.BlockSpec((None, 2, heads, HEAD_DIM), lambda b, s, pt: (b, 0, 0, 0))
    vspec = pl.BlockSpec((None, heads, VAL_DIM), lambda b, s, pt: (b, 0, 0))
    vec = pl.BlockSpec((1, HEAD_DIM), lambda b, s, pt: (0, 0))

    def page(shape, g):
        zeros = (0,) * (len(shape) - 1)
        return pl.BlockSpec((None,) + tuple(shape[1:]),
                            lambda b, s, pt: (pt[b * n_pages + s * n_grp + g],) + zeros)

    grid_spec = pltpu.PrefetchScalarGridSpec(
        num_scalar_prefetch=1,
        grid=(bsz, n_pages // n_grp),
        in_specs=[qspec, qspec, vspec, vec, vec, vec, vec, pl.BlockSpec((1, VAL_DIM), lambda b, s, pt: (0, 0))]
        + [page(cache_k.shape, g) for g in range(n_grp)] + [page(cache_v.shape, g) for g in range(n_grp)],
        out_specs=vspec,
        scratch_shapes=[pltpu.VMEM((2, heads, HEAD_DIM), F32), pltpu.VMEM((2, heads, HEAD_DIM), F32),
                        pltpu.VMEM((2, heads, VAL_DIM), F32)],
    )
    return pl.pallas_call(
        functools.partial(_attn_sample_kernel, n_grp=n_grp, lam_init=lam_init),
        grid_spec=grid_spec,
        out_shape=jax.ShapeDtypeStruct((bsz, heads, VAL_DIM), BF16),
        compiler_params=_params("parallel", "arbitrary"),
        name="attn_sample",
    )(page_table.reshape(-1), q, kn, vn, *lams, sub_norm_w.reshape(1, VAL_DIM),
      *([cache_k] * n_grp), *([cache_v] * n_grp))


def _lower_bound(lb_ref):
    p = lb_ref[...]
    e = jnp.exp(p - jnp.max(p, axis=0, keepdims=True))
    return e[0:1, :] / jnp.sum(e, axis=0, keepdims=True)


def _split3(x):
    hi = x.astype(BF16)
    r = x - hi.astype(F32)
    mid = r.astype(BF16)
    lo = (r - mid.astype(F32)).astype(BF16)
    return hi, mid, lo


def _hgrn_prompt_kernel(zq_ref, zf_ref, zi_ref, zg_ref, lb_ref, s0_ref, nw_ref, o_ref, sfin_ref,
                        q_s, k_s, b_s, oi_s, st_s, dec_s, *, seq):
    c = HGRN_CHUNK
    n_chunks = seq // c
    lb = _lower_bound(lb_ref)
    zf = zf_ref[...]
    f = lb + (1.0 - lb) * _sigmoid(zf)
    k_s[...] = (1.0 - lb) * _sigmoid(-zf)
    zq = zq_ref[...]
    q_s[...] = zq * _sigmoid(zq)
    logf = jnp.log(f)

    grp = 256
    r = lax.broadcasted_iota(jnp.int32, (grp, grp), 0)
    cc = lax.broadcasted_iota(jnp.int32, (grp, grp), 1)
    shift = c.bit_length() - 1
    same_chunk = jnp.right_shift(r, shift) == jnp.right_shift(cc, shift)
    tri = jnp.where(same_chunk & (cc <= r), 1.0, 0.0).astype(BF16)
    b_s[...] = logf
    for gi in range(seq // grp):
        rows = slice(gi * grp, (gi + 1) * grp)
        hi, mid, lo = _split3(b_s[rows, :])
        b_s[rows, :] = (jnp.dot(tri, hi, preferred_element_type=F32)
                        + jnp.dot(tri, mid, preferred_element_type=F32)
                        + jnp.dot(tri, lo, preferred_element_type=F32))

    half = c // 2
    hrow = lax.broadcasted_iota(jnp.int32, (half, RNN_DK), 0)

    def phase1(n, carry):
        rows = pl.ds(pl.multiple_of(n * c, c), c)
        b = b_s[rows, :]
        q = q_s[rows, :]
        k = k_s[rows, :]
        v = zi_ref[rows, :]
        b_last = b[c - 1:c, :]
        b_h, q_h = (b[:half], b[half:]), (q[:half], q[half:])
        o_h = [jnp.zeros((half, RNN_DV), F32), jnp.zeros((half, RNN_DV), F32)]
        for s in range(c):
            bs, ks, vs = b[s:s + 1, :], k[s:s + 1, :], v[s:s + 1, :]
            own = s // half
            keep = hrow >= s - own * half
            pw = jnp.where(keep, q_h[own] * ks * jnp.exp(jnp.where(keep, b_h[own] - bs, 0.0)), 0.0)
            o_h[own] = o_h[own] + jnp.sum(pw, axis=-1, keepdims=True) * vs
            if own == 0:
                pw = q_h[1] * ks * jnp.exp(b_h[1] - bs)
                o_h[1] = o_h[1] + jnp.sum(pw, axis=-1, keepdims=True) * vs
        oi_s[rows, :] = jnp.concatenate(o_h, axis=0)
        q_s[rows, :] = q * jnp.exp(b)
        kt = k * jnp.exp(b_last - b)
        st_s[n] = lax.dot_general(v.astype(BF16), kt.astype(BF16), TN_DIMS, preferred_element_type=F32)
        dec_s[pl.ds(n, 1), :] = jnp.exp(b_last)
        return carry

    lax.fori_loop(0, n_chunks, phase1, 0, unroll=4)

    def phase2(n, st):
        inc = st_s[n]
        st_s[n] = st
        return st * dec_s[pl.ds(n, 1), :] + inc

    st_fin = lax.fori_loop(0, n_chunks, phase2, s0_ref[...].T)
    sfin_ref[...] = st_fin.T

    def phase3(n, carry):
        rows = pl.ds(pl.multiple_of(n * c, c), c)
        oi_s[rows, :] += lax.dot_general(q_s[rows, :].astype(BF16), st_s[n].astype(BF16), NT_DIMS,
                                         preferred_element_type=F32)
        return carry

    lax.fori_loop(0, n_chunks, phase3, 0, unroll=8)
    o = oi_s[...]
    ms = jnp.mean(o * o, axis=-1, keepdims=True)
    zg = zg_ref[...]
    o_ref[...] = (o * lax.rsqrt(ms + NORM_EPS) * nw_ref[...] * (zg * _sigmoid(zg))).astype(o_ref.dtype)


def hgrn_prompt(z, lb_param, s0, rnn_norm_w, batch, seq, col0):
    m = z.shape[0]
    heads = s0.shape[1]
    zblk = lambda g: pl.BlockSpec((seq, RNN_DK), lambda b, h, g=g: (b, col0 + g * heads + h))
    n_chunks = seq // HGRN_CHUNK
    return pl.pallas_call(
        functools.partial(_hgrn_prompt_kernel, seq=seq),
        grid=(batch, heads),
        in_specs=[zblk(0), zblk(1), zblk(2), zblk(3),
                  pl.BlockSpec((lb_param.shape[0], RNN_DK), lambda b, h: (0, h)),
                  pl.BlockSpec((None, None, RNN_DK, RNN_DV), lambda b, h: (b, h, 0, 0)),
                  pl.BlockSpec((1, RNN_DV), lambda b, h: (0, 0))],
        out_specs=[pl.BlockSpec((seq, RNN_DV), lambda b, h: (b, h)),
                   pl.BlockSpec((None, None, RNN_DK, RNN_DV), lambda b, h: (b, h, 0, 0))],
        out_shape=[jax.ShapeDtypeStruct((m, heads * RNN_DV), BF16),
                   jax.ShapeDtypeStruct(s0.shape, F32)],
        scratch_shapes=[pltpu.VMEM((seq, RNN_DK), F32), pltpu.VMEM((seq, RNN_DK), F32),
                        pltpu.VMEM((seq, RNN_DK), F32), pltpu.VMEM((seq, RNN_DV), F32),
                        pltpu.VMEM((n_chunks, RNN_DV, RNN_DK), F32),
                        pltpu.VMEM((n_chunks, RNN_DK), F32)],
        compiler_params=_params("parallel", "parallel"),
        name="hgrn_prompt",
    )(z, z, z, z, lb_param, s0, rnn_norm_w.reshape(1, RNN_DV))


def _hgrn_sample_kernel(zq_ref, zf_ref, zi_ref, zg_ref, lb_ref, s0_ref, nw_ref, o_ref, s_ref):
    bsz = s0_ref.shape[0]
    lb = _lower_bound(lb_ref)
    zf = zf_ref[...]
    f_t = (lb + (1.0 - lb) * _sigmoid(zf)).T
    k_t = ((1.0 - lb) * _sigmoid(-zf)).T
    zq = zq_ref[...]
    q_t = (zq * _sigmoid(zq)).T
    for b in range(bsz):
        v = zi_ref[b:b + 1, :]
        s_new = f_t[:, b:b + 1] * s0_ref[b] + k_t[:, b:b + 1] * v
        s_ref[b] = s_new
        o = jnp.sum(q_t[:, b:b + 1] * s_new, axis=0, keepdims=True)
        ms = jnp.mean(o * o, axis=-1, keepdims=True)
        zg = zg_ref[b:b + 1, :]
        o_ref[b:b + 1, :] = (o * lax.rsqrt(ms + NORM_EPS) * nw_ref[...] * (zg * _sigmoid(zg))).astype(o_ref.dtype)


def hgrn_sample(z, lb_param, s0, rnn_norm_w, col0):
    bsz, heads = s0.shape[:2]
    zblk = lambda g: pl.BlockSpec((bsz, RNN_DK), lambda h, g=g: (0, col0 + g * heads + h))
    return pl.pallas_call(
        _hgrn_sample_kernel,
        grid=(heads,),
        in_specs=[zblk(0), zblk(1), zblk(2), zblk(3),
                  pl.BlockSpec((lb_param.shape[0], RNN_DK), lambda h: (0, h)),
                  pl.BlockSpec((bsz, None, RNN_DK, RNN_DV), lambda h: (0, h, 0, 0)),
                  pl.BlockSpec((1, RNN_DV), lambda h: (0, 0))],
        out_specs=[pl.BlockSpec((bsz, RNN_DV), lambda h: (0, h)),
                   pl.BlockSpec((bsz, None, RNN_DK, RNN_DV), lambda h: (0, h, 0, 0))],
        out_shape=[jax.ShapeDtypeStruct((bsz, heads * RNN_DV), BF16),
                   jax.ShapeDtypeStruct(s0.shape, F32)],
        compiler_params=_params("parallel"),
        name="hgrn_sample",
    )(z, z, z, z, lb_param, s0, rnn_norm_w.reshape(1, RNN_DV))


def _top16(s, ids):
    big = jnp.int32(2 ** 30)
    vals, sel = [], []
    for _ in range(PEER_TOPK):
        m = jnp.max(s, axis=0, keepdims=True)
        pick = jnp.min(jnp.where(s == m, ids, big), axis=0, keepdims=True)
        s = jnp.where(ids == pick, -jnp.inf, s)
        vals.append(m)
        sel.append(pick)
    return jnp.concatenate(vals, axis=0), jnp.concatenate(sel, axis=0)


def _peer_topk_kernel(*refs, cast_tables):
    if cast_tables:
        q_ref, sk_ref, u_ref, v_ref, a_ref, b_ref, g_ref, ub_ref, vb_ref = refs
        ub_ref[...] = u_ref[...].astype(BF16)
        vb_ref[...] = v_ref[...].astype(BF16)
    else:
        q_ref, sk_ref, a_ref, b_ref, g_ref = refs
    tm = q_ref.shape[0]
    heads = sk_ref.shape[0]
    key_ids = lax.broadcasted_iota(jnp.int32, (N_KEYS, tm), 0)
    k = PEER_TOPK
    for h in range(heads):
        half = []
        for c in range(2):
            col = (2 * h + c) * HEAD_DIM
            sc = lax.dot_general(sk_ref[h, c].astype(BF16), q_ref[:, col:col + HEAD_DIM].astype(BF16),
                                 NT_DIMS, preferred_element_type=F32)
            half.append(_top16(sc, key_ids))
        (s1, i1), (s2, i2) = half
        cand, eid, fid = [], [], []
        crow = lax.broadcasted_iota(jnp.int32, (k, tm), 0)
        for r in range(k // 2):
            nc = k if r == 0 else k // 2
            cand.append(s1[r:r + 1, :] + s2[:nc, :])
            eid.append(i1[r:r + 1, :] * N_KEYS + i2[:nc, :])
            fid.append(r * k + crow[:nc, :])
        cand.append(s1[k // 2:, :] + s2[0:1, :])
        eid.append(i1[k // 2:, :] * N_KEYS + i2[0:1, :])
        fid.append((k // 2 + crow[:k // 2, :]) * k)
        cand = jnp.concatenate(cand, axis=0)
        eid = jnp.concatenate(eid, axis=0)
        fid = jnp.concatenate(fid, axis=0)
        best, pick = _top16(cand, fid)
        e_sel = []
        for j in range(k):
            e_sel.append(jnp.max(jnp.where(fid == pick[j:j + 1, :], eid, -1), axis=0, keepdims=True))
        e_sel = jnp.concatenate(e_sel, axis=0)
        p = jnp.exp(best - best[0:1, :])
        gate = p / jnp.sum(p, axis=0, keepdims=True)
        rows = slice(h * k, (h + 1) * k)
        a_ref[rows, :] = jnp.right_shift(e_sel, LOG2_LANES)
        b_ref[rows, :] = jnp.bitwise_and(e_sel, N_KEYS - 1)
        g_ref[rows, :] = gate


def peer_topk(q, subkeys, tm, tables=None):
    m = q.shape[0]
    heads = subkeys.shape[0]
    n_sel = heads * PEER_TOPK
    steps = m // tm
    out = pl.BlockSpec((n_sel, tm), lambda i: (0, i))
    in_specs = [pl.BlockSpec((tm, q.shape[1]), lambda i: (i, 0)),
                pl.BlockSpec(subkeys.shape, lambda i: (0, 0, 0, 0))]
    out_specs = [out, out, out]
    out_shape = [jax.ShapeDtypeStruct((n_sel, m), jnp.int32),
                 jax.ShapeDtypeStruct((n_sel, m), jnp.int32),
                 jax.ShapeDtypeStruct((n_sel, m), F32)]
    args = [q, subkeys]
    if tables is not None:
        n_exp, d = tables[0].shape
        slab = pl.BlockSpec((n_exp // steps, d), lambda i: (i, 0))
        in_specs += [slab, slab]
        out_specs += [slab, slab]
        out_shape += [jax.ShapeDtypeStruct((n_exp, d), BF16)] * 2
        args += list(tables)
    return pl.pallas_call(
        functools.partial(_peer_topk_kernel, cast_tables=tables is not None),
        grid=(steps,),
        in_specs=in_specs,
        out_specs=out_specs,
        out_shape=out_shape,
        compiler_params=_params("parallel"),
        name="peer_topk",
    )(*args)


def _peer_gates_kernel(a_ref, b_ref, g_ref, o_ref):
    tg, n_sel = a_ref.shape
    ids = lax.broadcasted_iota(jnp.int32, (N_KEYS, n_sel), 0)

    def body(t, carry):
        row = pl.ds(t, 1)
        left = jnp.where(ids == a_ref[row, :], 1.0, 0.0).astype(BF16)
        right = jnp.where(ids == b_ref[row, :], g_ref[row, :], 0.0).astype(BF16)
        o_ref[t] = lax.dot_general(left, right, NT_DIMS, preferred_element_type=F32).astype(o_ref.dtype)
        return carry

    lax.fori_loop(0, tg, body, 0, unroll=16)


def peer_gates(a, b, g, tg):
    m, n_sel = a.shape
    spec = pl.BlockSpec((tg, n_sel), lambda i: (i, 0))
    return pl.pallas_call(
        _peer_gates_kernel,
        grid=(m // tg,),
        in_specs=[spec, spec, spec],
        out_specs=pl.BlockSpec((tg, N_KEYS, N_KEYS), lambda i: (i, 0, 0)),
        out_shape=jax.ShapeDtypeStruct((m, N_KEYS, N_KEYS), F32),
        compiler_params=_params("parallel"),
        name="peer_gates",
    )(a, b, g)


def _gelu(x):
    return 0.5 * x * (1.0 + lax.erf(x * (2.0 ** -0.5)))


def _peer_dense_kernel(x_ref, u_ref, v_ref, gs_ref, h_hbm, y_hbm, y_ref, *, tm):
    tile = pl.ds(pl.program_id(0) * tm, tm)

    @pl.when(pl.program_id(1) == 0)
    def _():
        pltpu.sync_copy(h_hbm.at[tile, :], y_ref)

    hid = lax.dot_general(x_ref[...], u_ref[...], NT_DIMS, preferred_element_type=F32)
    n_sub = hid.shape[1] // N_KEYS
    first = (pl.program_id(1) % (gs_ref.shape[1] // n_sub)) * n_sub
    gs2 = gs_ref.reshape(tm * GATE_ROWS, N_KEYS)
    w = []
    for j in range(n_sub):
        sl = slice(j * N_KEYS, (j + 1) * N_KEYS)
        gate = gs2[pl.ds(first + j, tm, stride=GATE_ROWS), :]
        w.append((gate * _gelu(hid[:, sl])).astype(BF16))
    w = jnp.concatenate(w, axis=-1)
    y_ref[...] += jnp.dot(w, v_ref[...], preferred_element_type=F32)

    @pl.when(pl.program_id(1) == pl.num_programs(1) - 1)
    def _():
        pltpu.sync_copy(y_ref, y_hbm.at[tile, :])


def peer_dense(xn, u_tab, v_tab, gates, h, tm, te):
    m, d = xn.shape
    n_exp = u_tab.shape[0]
    return pl.pallas_call(
        functools.partial(_peer_dense_kernel, tm=tm),
        grid=(m // tm, n_exp // te),
        in_specs=[pl.BlockSpec((tm, d), lambda i, e: (i, 0), pipeline_mode=pl.Buffered(1)),
                  pl.BlockSpec((te, d), lambda i, e: (e, 0)),
                  pl.BlockSpec((te, d), lambda i, e: (e, 0)),
                  pl.BlockSpec((tm, GATE_ROWS, N_KEYS), lambda i, e: (i, e // (GATE_ROWS * N_KEYS // te), 0)),
                  pl.BlockSpec(memory_space=pl.ANY)],
        out_specs=pl.BlockSpec(memory_space=pl.ANY),
        out_shape=jax.ShapeDtypeStruct((m, d), F32),
        scratch_shapes=[pltpu.VMEM((tm, d), F32)],
        compiler_params=pltpu.CompilerParams(dimension_semantics=("parallel", "arbitrary"),
                                             vmem_limit_bytes=PEER_DENSE_VMEM_LIMIT),
        name="peer_dense",
    )(xn, u_tab, v_tab, gates, h)


def _row_tile(m, cap):
    return min(m, cap)


def _trunk(x, attend, recur, w, lam_init):
    m, d = x.shape
    att_w = w["q_norm_cols"]
    xn = rmsnorm_bf16(x, w["norm1_w"], _row_tile(m, 256))
    z = matmul(xn, w["w_in"], tm=_row_tile(m, 1024), tn=512)
    qn, kn, knb, v, vb = qkv_post(z, w["q_norm_w"], w["k_norm_w"], att_w, _row_tile(m, 256))
    att = attend(qn, kn, knb, v, vb)
    rnn, s_fin = recur(z)
    mix = jnp.concatenate([att, rnn], axis=-1)
    h = matmul(mix, w["w_out"], x, tm=_row_tile(m, 1024), tn=512)
    hn = rmsnorm_bf16(h, w["norm2_w"], _row_tile(m, 256))
    pq = matmul(hn, w["peer_wq"], tm=_row_tile(m, 1024), tn=512)
    if "peer_tables_bf16" not in w:
        a, b, g, ub, vb16 = peer_topk(pq, w["peer_subkeys"], 128, tables=(w["peer_u"], w["peer_v"]))
        w["peer_tables_bf16"] = (ub, vb16)
    else:
        a, b, g = peer_topk(pq, w["peer_subkeys"], 128)
    gates = peer_gates(a.T, b.T, g.T, 64)
    y = peer_dense(hn, *w["peer_tables_bf16"], gates, h, _row_tile(m, 1024), 512)
    return y, kn, v, s_fin


def kernel(x_prompt, x_sample, cache_k, cache_v, state_rnn, page_table, norm1_w, w_in, q_norm_w, k_norm_w, lambda_q1, lambda_k1, lambda_q2, lambda_k2, sub_norm_w, lb_param, rnn_norm_w, w_out, norm2_w, peer_wq, peer_subkeys, peer_u, peer_v):
    depth = w_in.shape[0]
    assert depth == 1, "single-layer trunk"
    layer = 0
    batch, seq, d = x_prompt.shape
    n_dec, dec_seq, _ = x_sample.shape
    assert dec_seq == 1
    h_att = cache_k.shape[3]
    att_w = h_att * VAL_DIM
    h_rnn = state_rnn.shape[2]
    lam_init = 0.8 - 0.6 * math.exp(-0.3 * layer)
    rnn_col0 = 3 * att_w // LANES

    w = dict(norm1_w=norm1_w[layer], w_in=w_in[layer], q_norm_w=q_norm_w[layer], k_norm_w=k_norm_w[layer],
             w_out=w_out[layer], norm2_w=norm2_w[layer], peer_wq=peer_wq[layer],
             peer_subkeys=peer_subkeys[layer], peer_u=peer_u[layer], peer_v=peer_v[layer], q_norm_cols=att_w)
    lams = [p[layer].reshape(1, HEAD_DIM) for p in (lambda_q1, lambda_k1, lambda_q2, lambda_k2)]
    sub_w = sub_norm_w[layer]
    rnn_w = rnn_norm_w[layer]

    s0 = jnp.zeros((batch, h_rnn, RNN_DK, RNN_DV), F32)
    y_p, k_p, v_p, s_p = _trunk(
        x_prompt.reshape(batch * seq, d),
        lambda qn, kn, knb, v, vb: attn_prompt(qn, knb, vb, lams, sub_w, batch, seq, lam_init, ATTN_Q_BLOCK),
        lambda z: hgrn_prompt(z, lb_param, s0, rnn_w, batch, seq, rnn_col0),
        w, lam_init)

    m_pad = LANES
    xs = jnp.zeros((m_pad, d), F32).at[:n_dec].set(x_sample.reshape(n_dec, d))
    ck = cache_k[layer].reshape(cache_k.shape[1], PAGE_SIZE * h_att * 2, HEAD_DIM)
    cv = cache_v[layer]

    def attend_s(qn, kn, knb, v, vb):
        split = lambda a: a[:n_dec].astype(F32).reshape(n_dec, h_att, 2, HEAD_DIM).transpose(0, 2, 1, 3)
        o = attn_sample(split(qn), split(kn), v[:n_dec].reshape(n_dec, h_att, VAL_DIM), ck, cv, page_table,
                        lams, sub_w, lam_init, SAMPLE_PAGES_PER_STEP)
        return jnp.zeros((m_pad, att_w), BF16).at[:n_dec].set(o.reshape(n_dec, att_w))

    def recur_s(z):
        o, s_new = hgrn_sample(z, lb_param, state_rnn[layer], rnn_w, rnn_col0)
        return jnp.zeros((m_pad, o.shape[1]), BF16).at[:n_dec].set(o), s_new

    y_s, k_s, v_s, s_s = _trunk(xs, attend_s, recur_s, w, lam_init)

    return (y_p.reshape(batch, seq, d),
            y_s[:n_dec].reshape(n_dec, 1, d),
            k_p.reshape(1, batch, seq, h_att, 2, HEAD_DIM),
            v_p.reshape(1, batch, seq, h_att, VAL_DIM),
            k_s[:n_dec].reshape(1, n_dec, 1, h_att, 2, HEAD_DIM),
            v_s[:n_dec].reshape(1, n_dec, 1, h_att, VAL_DIM),
            s_p[None].astype(state_rnn.dtype),
            s_s[None].astype(state_rnn.dtype))
```

```python
import functools
import math

import jax
import jax.numpy as jnp
from jax import lax
from jax.experimental import pallas as pl
from jax.experimental.pallas import tpu as pltpu

F32 = jnp.float32
BF16 = jnp.bfloat16

LANES = 128
LOG2_LANES = 7
HEAD_DIM = 128
VAL_DIM = 2 * HEAD_DIM
RNN_DK = 128
RNN_DV = 128
HGRN_CHUNK = 128
NORM_EPS = 1e-6
ATTN_SCALE = HEAD_DIM ** -0.5
LOG2_E = math.log2(math.e)
N_KEYS = 128
PEER_TOPK = 16
PAGE_SIZE = 128
GATE_ROWS = 8
SAMPLE_PAGES_PER_STEP = 8
ATTN_HEADS_PER_STEP = 2
ATTN_Q_BLOCK = 256
ATTN_KV_BLOCK = 512
VMEM_LIMIT = 56 * 1024 * 1024
PEER_DENSE_VMEM_LIMIT = 60 * 1024 * 1024

NT_DIMS = (((1,), (1,)), ((), ()))
TN_DIMS = (((0,), (0,)), ((), ()))


def _params(*sem):
    return pltpu.CompilerParams(dimension_semantics=sem, vmem_limit_bytes=VMEM_LIMIT)


def _sigmoid(x):
    return 1.0 / (1.0 + jnp.exp(-x))


def _rmsnorm_kernel(x_ref, w_ref, o_ref):
    x = x_ref[...]
    ms = jnp.mean(x * x, axis=-1, keepdims=True)
    o_ref[...] = (x * lax.rsqrt(ms + NORM_EPS) * w_ref[...]).astype(o_ref.dtype)


def rmsnorm_bf16(x, w, tm):
    m, d = x.shape
    return pl.pallas_call(
        _rmsnorm_kernel,
        grid=(m // tm,),
        in_specs=[pl.BlockSpec((tm, d), lambda i: (i, 0)),
                  pl.BlockSpec((1, d), lambda i: (0, 0))],
        out_specs=pl.BlockSpec((tm, d), lambda i: (i, 0)),
        out_shape=jax.ShapeDtypeStruct((m, d), BF16),
        compiler_params=_params("parallel"),
        name="rmsnorm",
    )(x, w.reshape(1, d))


def _matmul_kernel(*refs, has_res):
    if has_res:
        x_ref, w_ref, r_ref, o_ref, wb_ref = refs
    else:
        x_ref, w_ref, o_ref, wb_ref = refs

    @pl.when(pl.program_id(1) == 0)
    def _():
        wb_ref[...] = w_ref[...].astype(BF16)

    acc = jnp.dot(x_ref[...], wb_ref[...], preferred_element_type=F32)
    if has_res:
        acc = r_ref[...] + acc
    o_ref[...] = acc


def matmul(x, w, res=None, *, tm, tn):
    m, k = x.shape
    n = w.shape[1]
    in_specs = [pl.BlockSpec((tm, k), lambda j, i: (i, 0)),
                pl.BlockSpec((k, tn), lambda j, i: (0, j))]
    args = [x, w]
    if res is not None:
        in_specs.append(pl.BlockSpec((tm, tn), lambda j, i: (i, j)))
        args.append(res)
    return pl.pallas_call(
        functools.partial(_matmul_kernel, has_res=res is not None),
        grid=(n // tn, m // tm),
        in_specs=in_specs,
        out_specs=pl.BlockSpec((tm, tn), lambda j, i: (i, j)),
        out_shape=jax.ShapeDtypeStruct((m, n), F32),
        scratch_shapes=[pltpu.VMEM((k, tn), BF16)],
        compiler_params=_params("arbitrary", "arbitrary"),
        name="matmul",
    )(*args)


def _qkv_kernel(zq_ref, zk_ref, zv_ref, qw_ref, kw_ref, qn_ref, kn_ref, knb_ref, v_ref, vb_ref):
    width = zq_ref.shape[1]
    for g in range(width // HEAD_DIM):
        sl = slice(g * HEAD_DIM, (g + 1) * HEAD_DIM)
        xq = zq_ref[:, sl]
        yq = xq * lax.rsqrt(jnp.mean(xq * xq, axis=-1, keepdims=True) + NORM_EPS) * qw_ref[...]
        qn_ref[:, sl] = yq.astype(BF16)
        xk = zk_ref[:, sl]
        yk = xk * lax.rsqrt(jnp.mean(xk * xk, axis=-1, keepdims=True) + NORM_EPS) * kw_ref[...]
        kn_ref[:, sl] = yk
        knb_ref[:, sl] = yk.astype(BF16)
    v = zv_ref[...]
    v_ref[...] = v
    vb_ref[...] = v.astype(BF16)


def qkv_post(z, q_norm_w, k_norm_w, width, tm):
    m = z.shape[0]
    blk = lambda c: pl.BlockSpec((tm, width), lambda i, c=c: (i, c))
    wspec = pl.BlockSpec((1, HEAD_DIM), lambda i: (0, 0))
    out = lambda dt: jax.ShapeDtypeStruct((m, width), dt)
    return pl.pallas_call(
        _qkv_kernel,
        grid=(m // tm,),
        in_specs=[blk(0), blk(1), blk(2), wspec, wspec],
        out_specs=[blk(0)] * 5,
        out_shape=[out(BF16), out(F32), out(BF16), out(F32), out(BF16)],
        compiler_params=_params("parallel"),
        name="qkv_post",
    )(z, z, z, q_norm_w.reshape(1, HEAD_DIM), k_norm_w.reshape(1, HEAD_DIM))


def _lambda(lq1_ref, lk1_ref, lq2_ref, lk2_ref, lam_init):
    a = jnp.sum(lq1_ref[...] * lk1_ref[...], axis=-1, keepdims=True)
    b = jnp.sum(lq2_ref[...] * lk2_ref[...], axis=-1, keepdims=True)
    return jnp.exp(a) - jnp.exp(b) + lam_init


def _sub_norm(o, w, lam_init):
    ms = jnp.mean(o * o, axis=-1, keepdims=True)
    return o * lax.rsqrt(ms + NORM_EPS) * w * (1.0 - lam_init)


def _attn_prompt_kernel(q_ref, k_ref, v_ref, lq1_ref, lk1_ref, lq2_ref, lk2_ref, sw_ref, o_ref,
                        m_ref, l_ref, acc_ref, *, tq, lam_init):
    qi = pl.program_id(2)
    m_ref[...] = jnp.full(m_ref.shape, -jnp.inf, F32)
    l_ref[...] = jnp.zeros(l_ref.shape, F32)
    acc_ref[...] = jnp.zeros(acc_ref.shape, F32)
    tk = ATTN_KV_BLOCK
    n_full = (qi * tq + 1) // tk

    n_chain = m_ref.shape[0]

    def block(j, masked):
        kv_rows = pl.ds(pl.multiple_of(j * tk, tk), tk)
        for ch in range(n_chain):
            sl = slice(ch * HEAD_DIM, (ch + 1) * HEAD_DIM)
            vsl = slice(ch // 2 * VAL_DIM, (ch // 2 + 1) * VAL_DIM)
            s = lax.dot_general(q_ref[:, sl], k_ref[kv_rows, sl], NT_DIMS,
                                preferred_element_type=F32) * (ATTN_SCALE * LOG2_E)
            if masked:
                row = qi * tq + lax.broadcasted_iota(jnp.int32, (tq, tk), 0)
                col = j * tk + lax.broadcasted_iota(jnp.int32, (tq, tk), 1)
                s = jnp.where(col <= row, s, -jnp.inf)
            m_old = m_ref[ch]
            m_new = jnp.maximum(m_old, jnp.max(s, axis=-1, keepdims=True))
            alpha = jnp.exp2(m_old - m_new)
            p = jnp.exp2(s - m_new)
            p_lanes = sum(p[:, t * LANES:(t + 1) * LANES] for t in range(1, tk // LANES)) + p[:, :LANES]
            l_ref[ch] = alpha * l_ref[ch] + p_lanes
            acc_ref[ch] = alpha * acc_ref[ch] + jnp.dot(p.astype(BF16), v_ref[kv_rows, vsl],
                                                        preferred_element_type=F32)
            m_ref[ch] = m_new

    def full_block(j, carry):
        block(j, masked=False)
        return carry

    lax.fori_loop(0, n_full, full_block, 0)
    block(n_full, masked=True)
    lam = _lambda(lq1_ref, lk1_ref, lq2_ref, lk2_ref, lam_init)
    for hh in range(n_chain // 2):
        l0 = jnp.sum(l_ref[2 * hh], axis=-1, keepdims=True)
        l1 = jnp.sum(l_ref[2 * hh + 1], axis=-1, keepdims=True)
        o = acc_ref[2 * hh] / l0 - lam * (acc_ref[2 * hh + 1] / l1)
        o_ref[:, hh * VAL_DIM:(hh + 1) * VAL_DIM] = _sub_norm(o, sw_ref[...], lam_init).astype(o_ref.dtype)


def attn_prompt(qn, knb, vb, lams, sub_norm_w, batch, seq, lam_init, tq):
    m, width = qn.shape
    hp = ATTN_HEADS_PER_STEP
    blk_w = hp * VAL_DIM
    nq = seq // tq
    vec = pl.BlockSpec((1, HEAD_DIM), lambda b, h, i: (0, 0))
    return pl.pallas_call(
        functools.partial(_attn_prompt_kernel, tq=tq, lam_init=lam_init),
        grid=(batch, width // blk_w, nq),
        in_specs=[pl.BlockSpec((tq, blk_w), lambda b, h, i: (b * nq + i, h)),
                  pl.BlockSpec((seq, blk_w), lambda b, h, i: (b, h)),
                  pl.BlockSpec((seq, blk_w), lambda b, h, i: (b, h)),
                  vec, vec, vec, vec,
                  pl.BlockSpec((1, VAL_DIM), lambda b, h, i: (0, 0))],
        out_specs=pl.BlockSpec((tq, blk_w), lambda b, h, i: (b * nq + i, h)),
        out_shape=jax.ShapeDtypeStruct((m, width), BF16),
        scratch_shapes=[pltpu.VMEM((2 * hp, tq, 1), F32), pltpu.VMEM((2 * hp, tq, LANES), F32),
                        pltpu.VMEM((2 * hp, tq, VAL_DIM), F32)],
        compiler_params=_params("parallel", "parallel", "arbitrary"),
        name="attn_prompt",
    )(qn, knb, vb, *lams, sub_norm_w.reshape(1, VAL_DIM))


def _attn_sample_kernel(pt_ref, q_ref, kn_ref, vn_ref, lq1_ref, lk1_ref, lq2_ref, lk2_ref, sw_ref, *rest,
                        n_grp, lam_init):
    del pt_ref
    k_refs, v_refs = rest[:n_grp], rest[n_grp:2 * n_grp]
    o_ref, m_ref, l_ref, acc_ref = rest[2 * n_grp:]
    step = pl.program_id(1)
    heads = q_ref.shape[1]

    @pl.when(step == 0)
    def _():
        m_ref[...] = jnp.full(m_ref.shape, -jnp.inf, F32)
        l_ref[...] = jnp.zeros(l_ref.shape, F32)
        acc_ref[...] = jnp.zeros(acc_ref.shape, F32)

    q = [q_ref[c] * (ATTN_SCALE * LOG2_E) for c in range(2)]
    rows = PAGE_SIZE * heads
    r_blk = jnp.right_shift(lax.broadcasted_iota(jnp.int32, (VAL_DIM, VAL_DIM), 0), LOG2_LANES)
    c_blk = jnp.right_shift(lax.broadcasted_iota(jnp.int32, (VAL_DIM, VAL_DIM), 1), LOG2_LANES)
    ones_blk = jnp.where(r_blk == c_blk, 1.0, 0.0).astype(BF16)
    scores = []
    for g in range(n_grp):
        prod = [(k_refs[g][pl.ds(c, rows, stride=2), :].reshape(PAGE_SIZE, heads, HEAD_DIM) * q[c][None])
                .reshape(rows, HEAD_DIM) for c in range(2)]
        lhs = jnp.concatenate(prod, axis=-1).astype(BF16)
        scores.append(jnp.dot(lhs, ones_blk, preferred_element_type=F32).reshape(PAGE_SIZE, heads, VAL_DIM))
    for c in range(2):
        half = slice(c * HEAD_DIM, (c + 1) * HEAD_DIM)
        m_old = m_ref[c]
        m_new = m_old
        for s in scores:
            m_new = jnp.maximum(m_new, jnp.max(s[:, :, half], axis=0))
        alpha = jnp.exp2(m_old - m_new)
        l_new = alpha * l_ref[c]
        acc = jnp.concatenate([alpha, alpha], axis=-1) * acc_ref[c]
        for g in range(n_grp):
            p = jnp.exp2(scores[g][:, :, half] - m_new[None])
            l_new = l_new + jnp.sum(p, axis=0)
            acc = acc + jnp.sum(jnp.concatenate([p, p], axis=-1) * v_refs[g][...], axis=0)
        m_ref[c] = m_new
        l_ref[c] = l_new
        acc_ref[c] = acc

    @pl.when(step == pl.num_programs(1) - 1)
    def _():
        outs = []
        for c in range(2):
            s_new = jnp.sum(kn_ref[c] * q[c], axis=-1, keepdims=True)
            m_old = m_ref[c]
            m_new = jnp.maximum(m_old, s_new)
            alpha = jnp.exp2(m_old - m_new)
            p = jnp.exp2(s_new - m_new)
            wide = lambda a: jnp.concatenate([a, a], axis=-1)
            outs.append((wide(alpha) * acc_ref[c] + wide(p) * vn_ref[...]) / wide(alpha * l_ref[c] + p))
        lam = _lambda(lq1_ref, lk1_ref, lq2_ref, lk2_ref, lam_init)
        o_ref[...] = _sub_norm(outs[0] - lam * outs[1], sw_ref[...], lam_init).astype(o_ref.dtype)


def attn_sample(q, kn, vn, cache_k, cache_v, page_table, lams, sub_norm_w, lam_init, n_grp):
    bsz, _, heads, _ = q.shape
    n_pages = page_table.shape[1]
    qspec = pl.BlockSpec((None, 2, heads, HEAD_DIM), lambda b, s, pt: (b, 0, 0, 0))
    vspec = pl.BlockSpec((None, heads, VAL_DIM), lambda b, s, pt: (b, 0, 0))
    vec = pl.BlockSpec((1, HEAD_DIM), lambda b, s, pt: (0, 0))

    def page(shape, g):
        zeros = (0,) * (len(shape) - 1)
        return pl.BlockSpec((None,) + tuple(shape[1:]),
                            lambda b, s, pt: (pt[b * n_pages + s * n_grp + g],) + zeros)

    grid_spec = pltpu.PrefetchScalarGridSpec(
        num_scalar_prefetch=1,
        grid=(bsz, n_pages // n_grp),
        in_specs=[qspec, qspec, vspec, vec, vec, vec, vec, pl.BlockSpec((1, VAL_DIM), lambda b, s, pt: (0, 0))]
        + [page(cache_k.shape, g) for g in range(n_grp)] + [page(cache_v.shape, g) for g in range(n_grp)],
        out_specs=vspec,
        scratch_shapes=[pltpu.VMEM((2, heads, HEAD_DIM), F32), pltpu.VMEM((2, heads, HEAD_DIM), F32),
                        pltpu.VMEM((2, heads, VAL_DIM), F32)],
    )
    return pl.pallas_call(
        functools.partial(_attn_sample_kernel, n_grp=n_grp, lam_init=lam_init),
        grid_spec=grid_spec,
        out_shape=jax.ShapeDtypeStruct((bsz, heads, VAL_DIM), BF16),
        compiler_params=_params("parallel", "arbitrary"),
        name="attn_sample",
    )(page_table.reshape(-1), q, kn, vn, *lams, sub_norm_w.reshape(1, VAL_DIM),
      *([cache_k] * n_grp), *([cache_v] * n_grp))


def _lower_bound(lb_ref):
    p = lb_ref[...]
    e = jnp.exp(p - jnp.max(p, axis=0, keepdims=True))
    return e[0:1, :] / jnp.sum(e, axis=0, keepdims=True)


def _split3(x):
    hi = x.astype(BF16)
    r = x - hi.astype(F32)
    mid = r.astype(BF16)
    lo = (r - mid.astype(F32)).astype(BF16)
    return hi, mid, lo


def _hgrn_prompt_kernel(zq_ref, zf_ref, zi_ref, zg_ref, lb_ref, s0_ref, nw_ref, o_ref, sfin_ref,
                        q_s, k_s, b_s, f_s, oi_s, st_s, dec_s, *, seq):
    c = HGRN_CHUNK
    n_chunks = seq // c
    lb = _lower_bound(lb_ref)
    zf = zf_ref[...]
    sig = _sigmoid(zf)
    f = lb + (1.0 - lb) * sig
    f_s[...] = f
    k_s[...] = (1.0 - lb) * (1.0 - sig)
    zq = zq_ref[...]
    q_s[...] = zq * _sigmoid(zq)

    grp = 256
    r = lax.broadcasted_iota(jnp.int32, (grp, grp), 0)
    cc = lax.broadcasted_iota(jnp.int32, (grp, grp), 1)
    shift = c.bit_length() - 1
    same_chunk = jnp.right_shift(r, shift) == jnp.right_shift(cc, shift)
    tri = jnp.where(same_chunk & (cc <= r), 1.0, 0.0).astype(BF16)
    b_s[...] = jnp.log2(f)
    for gi in range(seq // grp):
        rows = slice(gi * grp, (gi + 1) * grp)
        hi, mid, lo = _split3(b_s[rows, :])
        b_s[rows, :] = (jnp.dot(tri, hi, preferred_element_type=F32)
                        + jnp.dot(tri, mid, preferred_element_type=F32)
                        + jnp.dot(tri, lo, preferred_element_type=F32))

    trow = lax.broadcasted_iota(jnp.int32, (c, RNN_DK), 0)
    arow = lax.broadcasted_iota(jnp.int32, (c, c), 0)
    acol = lax.broadcasted_iota(jnp.int32, (c, c), 1)
    sublane = 8

    def phase1(n, carry):
        base = pl.multiple_of(n * c, c)
        rows = pl.ds(base, c)
        b = b_s[rows, :]
        q = q_s[rows, :]
        k = k_s[rows, :]
        v = zi_ref[rows, :]
        b_last = b[c - 1:c, :]

        def b_row(r, nrows):
            return jnp.broadcast_to(b_s[pl.ds(base + r, 1), :], (nrows, RNN_DK))

        att = jnp.where(arow == acol, jnp.sum(q * k, axis=-1, keepdims=True), 0.0)
        m = 1
        while m < c:
            blk = 2 * m
            if m == 1:
                odd = (trow & 1) != 0
                qh = jnp.where(odd, q * f_s[rows, :], 0.0)
                kh = jnp.where(odd, 0.0, k)
            elif m < sublane:
                cands = [jnp.concatenate([b_row(t0 + i * blk + m - 1, sublane) for t0 in range(0, c, sublane)], axis=0)
                         for i in range(sublane // blk)]
                ref = cands[-1]
                for i in range(sublane // blk - 2, -1, -1):
                    ref = jnp.where((trow & (sublane - 1)) < (i + 1) * blk, cands[i], ref)
                e = b - ref
                x = jnp.exp2(jnp.minimum(e, -e))
                upper = (trow & m) != 0
                qh = jnp.where(upper, q * x, 0.0)
                kh = jnp.where(upper, 0.0, k * x)
            else:
                zeros = jnp.zeros((m, RNN_DK), F32)
                q_parts, k_parts = [], []
                for s0 in range(0, c, blk):
                    lo, up = slice(s0, s0 + m), slice(s0 + m, s0 + blk)
                    ref = b_row(s0 + m - 1, m)
                    q_parts += [zeros, q[up] * jnp.exp2(b[up] - ref)]
                    k_parts += [k[lo] * jnp.exp2(ref - b[lo]), zeros]
                qh = jnp.concatenate(q_parts, axis=0)
                kh = jnp.concatenate(k_parts, axis=0)
            a = lax.dot_general(qh.astype(BF16), kh.astype(BF16), NT_DIMS, preferred_element_type=F32)
            if blk < c:
                shift_blk = blk.bit_length() - 1
                a = jnp.where(jnp.right_shift(arow, shift_blk) == jnp.right_shift(acol, shift_blk), a, 0.0)
            att = att + a
            m = blk
        oi_s[rows, :] = jnp.dot(att.astype(BF16), v.astype(BF16), preferred_element_type=F32)
        q_s[rows, :] = q * jnp.exp2(b)
        kt = k * jnp.exp2(b_last - b)
        st_s[n] = lax.dot_general(v.astype(BF16), kt.astype(BF16), TN_DIMS, preferred_element_type=F32)
        dec_s[pl.ds(n, 1), :] = jnp.exp2(b_last)
        return carry

    lax.fori_loop(0, n_chunks, phase1, 0, unroll=2)

    def phase2(n, st):
        inc = st_s[n]
        st_s[n] = st
        return st * dec_s[pl.ds(n, 1), :] + inc

    st_fin = lax.fori_loop(0, n_chunks, phase2, s0_ref[...].T)
    sfin_ref[...] = st_fin.T

    def phase3(n, carry):
        rows = pl.ds(pl.multiple_of(n * c, c), c)
        oi_s[rows, :] += lax.dot_general(q_s[rows, :].astype(BF16), st_s[n].astype(BF16), NT_DIMS,
                                         preferred_element_type=F32)
        return carry

    lax.fori_loop(0, n_chunks, phase3, 0, unroll=4)
    o = oi_s[...]
    ms = jnp.mean(o * o, axis=-1, keepdims=True)
    zg = zg_ref[...]
    o_ref[...] = (o * lax.rsqrt(ms + NORM_EPS) * nw_ref[...] * (zg * _sigmoid(zg))).astype(o_ref.dtype)


def hgrn_prompt(z, lb_param, s0, rnn_norm_w, batch, seq, col0):
    m = z.shape[0]
    heads = s0.shape[1]
    zblk = lambda g: pl.BlockSpec((seq, RNN_DK), lambda b, h, g=g: (b, col0 + g * heads + h))
    n_chunks = seq // HGRN_CHUNK
    return pl.pallas_call(
        functools.partial(_hgrn_prompt_kernel, seq=seq),
        grid=(batch, heads),
        in_specs=[zblk(0), zblk(1), zblk(2), zblk(3),
                  pl.BlockSpec((lb_param.shape[0], RNN_DK), lambda b, h: (0, h)),
                  pl.BlockSpec((None, None, RNN_DK, RNN_DV), lambda b, h: (b, h, 0, 0)),
                  pl.BlockSpec((1, RNN_DV), lambda b, h: (0, 0))],
        out_specs=[pl.BlockSpec((seq, RNN_DV), lambda b, h: (b, h)),
                   pl.BlockSpec((None, None, RNN_DK, RNN_DV), lambda b, h: (b, h, 0, 0))],
        out_shape=[jax.ShapeDtypeStruct((m, heads * RNN_DV), BF16),
                   jax.ShapeDtypeStruct(s0.shape, F32)],
        scratch_shapes=[pltpu.VMEM((seq, RNN_DK), F32), pltpu.VMEM((seq, RNN_DK), F32),
                        pltpu.VMEM((seq, RNN_DK), F32), pltpu.VMEM((seq, RNN_DK), F32),
                        pltpu.VMEM((seq, RNN_DV), F32),
                        pltpu.VMEM((n_chunks, RNN_DV, RNN_DK), F32),
                        pltpu.VMEM((n_chunks, RNN_DK), F32)],
        compiler_params=_params("parallel", "parallel"),
        name="hgrn_prompt",
    )(z, z, z, z, lb_param, s0, rnn_norm_w.reshape(1, RNN_DV))


def _hgrn_sample_kernel(zq_ref, zf_ref, zi_ref, zg_ref, lb_ref, s0_ref, nw_ref, o_ref, s_ref):
    bsz = s0_ref.shape[0]
    lb = _lower_bound(lb_ref)
    zf = zf_ref[...]
    f_t = (lb + (1.0 - lb) * _sigmoid(zf)).T
    k_t = ((1.0 - lb) * _sigmoid(-zf)).T
    zq = zq_ref[...]
    q_t = (zq * _sigmoid(zq)).T
    for b in range(bsz):
        v = zi_ref[b:b + 1, :]
        s_new = f_t[:, b:b + 1] * s0_ref[b] + k_t[:, b:b + 1] * v
        s_ref[b] = s_new
        o = jnp.sum(q_t[:, b:b + 1] * s_new, axis=0, keepdims=True)
        ms = jnp.mean(o * o, axis=-1, keepdims=True)
        zg = zg_ref[b:b + 1, :]
        o_ref[b:b + 1, :] = (o * lax.rsqrt(ms + NORM_EPS) * nw_ref[...] * (zg * _sigmoid(zg))).astype(o_ref.dtype)


def hgrn_sample(z, lb_param, s0, rnn_norm_w, col0):
    bsz, heads = s0.shape[:2]
    zblk = lambda g: pl.BlockSpec((bsz, RNN_DK), lambda h, g=g: (0, col0 + g * heads + h))
    return pl.pallas_call(
        _hgrn_sample_kernel,
        grid=(heads,),
        in_specs=[zblk(0), zblk(1), zblk(2), zblk(3),
                  pl.BlockSpec((lb_param.shape[0], RNN_DK), lambda h: (0, h)),
                  pl.BlockSpec((bsz, None, RNN_DK, RNN_DV), lambda h: (0, h, 0, 0)),
                  pl.BlockSpec((1, RNN_DV), lambda h: (0, 0))],
        out_specs=[pl.BlockSpec((bsz, RNN_DV), lambda h: (0, h)),
                   pl.BlockSpec((bsz, None, RNN_DK, RNN_DV), lambda h: (0, h, 0, 0))],
        out_shape=[jax.ShapeDtypeStruct((bsz, heads * RNN_DV), BF16),
                   jax.ShapeDtypeStruct(s0.shape, F32)],
        compiler_params=_params("parallel"),
        name="hgrn_sample",
    )(z, z, z, z, lb_param, s0, rnn_norm_w.reshape(1, RNN_DV))


def _top16(s, ids):
    big = jnp.int32(2 ** 30)
    vals, sel = [], []
    for _ in range(PEER_TOPK):
        m = jnp.max(s, axis=0, keepdims=True)
        pick = jnp.min(jnp.where(s == m, ids, big), axis=0, keepdims=True)
        s = jnp.where(ids == pick, -jnp.inf, s)
        vals.append(m)
        sel.append(pick)
    return jnp.concatenate(vals, axis=0), jnp.concatenate(sel, axis=0)


def _peer_topk_kernel(*refs, cast_tables):
    if cast_tables:
        q_ref, sk_ref, u_ref, v_ref, a_ref, b_ref, g_ref, ub_ref, vb_ref = refs
        ub_ref[...] = u_ref[...].astype(BF16)
        vb_ref[...] = v_ref[...].astype(BF16)
    else:
        q_ref, sk_ref, a_ref, b_ref, g_ref = refs
    tm = q_ref.shape[0]
    heads = sk_ref.shape[0]
    key_ids = lax.broadcasted_iota(jnp.int32, (N_KEYS, tm), 0)
    k = PEER_TOPK
    for h in range(heads):
        half = []
        for c in range(2):
            col = (2 * h + c) * HEAD_DIM
            sc = lax.dot_general(sk_ref[h, c].astype(BF16), q_ref[:, col:col + HEAD_DIM].astype(BF16),
                                 NT_DIMS, preferred_element_type=F32)
            half.append(_top16(sc, key_ids))
        (s1, i1), (s2, i2) = half
        cand, eid, fid = [], [], []
        crow = lax.broadcasted_iota(jnp.int32, (k, tm), 0)
        for r in range(k // 2):
            nc = k if r == 0 else k // 2
            cand.append(s1[r:r + 1, :] + s2[:nc, :])
            eid.append(i1[r:r + 1, :] * N_KEYS + i2[:nc, :])
            fid.append(r * k + crow[:nc, :])
        cand.append(s1[k // 2:, :] + s2[0:1, :])
        eid.append(i1[k // 2:, :] * N_KEYS + i2[0:1, :])
        fid.append((k // 2 + crow[:k // 2, :]) * k)
        cand = jnp.concatenate(cand, axis=0)
        eid = jnp.concatenate(eid, axis=0)
        fid = jnp.concatenate(fid, axis=0)
        best, pick = _top16(cand, fid)
        e_sel = []
        for j in range(k):
            e_sel.append(jnp.max(jnp.where(fid == pick[j:j + 1, :], eid, -1), axis=0, keepdims=True))
        e_sel = jnp.concatenate(e_sel, axis=0)
        p = jnp.exp(best - best[0:1, :])
        gate = p / jnp.sum(p, axis=0, keepdims=True)
        rows = slice(h * k, (h + 1) * k)
        a_ref[rows, :] = jnp.right_shift(e_sel, LOG2_LANES)
        b_ref[rows, :] = jnp.bitwise_and(e_sel, N_KEYS - 1)
        g_ref[rows, :] = gate


def peer_topk(q, subkeys, tm, tables=None):
    m = q.shape[0]
    heads = subkeys.shape[0]
    n_sel = heads * PEER_TOPK
    steps = m // tm
    out = pl.BlockSpec((n_sel, tm), lambda i: (0, i))
    in_specs = [pl.BlockSpec((tm, q.shape[1]), lambda i: (i, 0)),
                pl.BlockSpec(subkeys.shape, lambda i: (0, 0, 0, 0))]
    out_specs = [out, out, out]
    out_shape = [jax.ShapeDtypeStruct((n_sel, m), jnp.int32),
                 jax.ShapeDtypeStruct((n_sel, m), jnp.int32),
                 jax.ShapeDtypeStruct((n_sel, m), F32)]
    args = [q, subkeys]
    if tables is not None:
        n_exp, d = tables[0].shape
        slab = pl.BlockSpec((n_exp // steps, d), lambda i: (i, 0))
        in_specs += [slab, slab]
        out_specs += [slab, slab]
        out_shape += [jax.ShapeDtypeStruct((n_exp, d), BF16)] * 2
        args += list(tables)
    return pl.pallas_call(
        functools.partial(_peer_topk_kernel, cast_tables=tables is not None),
        grid=(steps,),
        in_specs=in_specs,
        out_specs=out_specs,
        out_shape=out_shape,
        compiler_params=_params("parallel"),
        name="peer_topk",
    )(*args)


def _peer_gates_kernel(a_ref, b_ref, g_ref, o_ref):
    tg, n_sel = a_ref.shape
    ids = lax.broadcasted_iota(jnp.int32, (N_KEYS, n_sel), 0)

    def body(t, carry):
        row = pl.ds(t, 1)
        left = jnp.where(ids == a_ref[row, :], 1.0, 0.0).astype(BF16)
        right = jnp.where(ids == b_ref[row, :], g_ref[row, :], 0.0).astype(BF16)
        o_ref[t] = lax.dot_general(left, right, NT_DIMS, preferred_element_type=F32).astype(o_ref.dtype)
        return carry

    lax.fori_loop(0, tg, body, 0, unroll=16)


def peer_gates(a, b, g, tg):
    m, n_sel = a.shape
    spec = pl.BlockSpec((tg, n_sel), lambda i: (i, 0))
    return pl.pallas_call(
        _peer_gates_kernel,
        grid=(m // tg,),
        in_specs=[spec, spec, spec],
        out_specs=pl.BlockSpec((tg, N_KEYS, N_KEYS), lambda i: (i, 0, 0)),
        out_shape=jax.ShapeDtypeStruct((m, N_KEYS, N_KEYS), F32),
        compiler_params=_params("parallel"),
        name="peer_gates",
    )(a, b, g)


def _gelu(x):
    return 0.5 * x * (1.0 + lax.erf(x * (2.0 ** -0.5)))


def _peer_dense_kernel(x_ref, u_ref, v_ref, gs_ref, h_hbm, y_hbm, y_ref, *, tm):
    tile = pl.ds(pl.program_id(0) * tm, tm)

    @pl.when(pl.program_id(1) == 0)
    def _():
        pltpu.sync_copy(h_hbm.at[tile, :], y_ref)

    hid = lax.dot_general(x_ref[...], u_ref[...], NT_DIMS, preferred_element_type=F32)
    n_sub = hid.shape[1] // N_KEYS
    first = (pl.program_id(1) % (gs_ref.shape[1] // n_sub)) * n_sub
    gs2 = gs_ref.reshape(tm * GATE_ROWS, N_KEYS)
    w = []
    for j in range(n_sub):
        sl = slice(j * N_KEYS, (j + 1) * N_KEYS)
        gate = gs2[pl.ds(first + j, tm, stride=GATE_ROWS), :]
        w.append((gate * _gelu(hid[:, sl])).astype(BF16))
    w = jnp.concatenate(w, axis=-1)
    y_ref[...] += jnp.dot(w, v_ref[...], preferred_element_type=F32)

    @pl.when(pl.program_id(1) == pl.num_programs(1) - 1)
    def _():
        pltpu.sync_copy(y_ref, y_hbm.at[tile, :])


def peer_dense(xn, u_tab, v_tab, gates, h, tm, te):
    m, d = xn.shape
    n_exp = u_tab.shape[0]
    return pl.pallas_call(
        functools.partial(_peer_dense_kernel, tm=tm),
        grid=(m // tm, n_exp // te),
        in_specs=[pl.BlockSpec((tm, d), lambda i, e: (i, 0), pipeline_mode=pl.Buffered(1)),
                  pl.BlockSpec((te, d), lambda i, e: (e, 0)),
                  pl.BlockSpec((te, d), lambda i, e: (e, 0)),
                  pl.BlockSpec((tm, GATE_ROWS, N_KEYS), lambda i, e: (i, e // (GATE_ROWS * N_KEYS // te), 0)),
                  pl.BlockSpec(memory_space=pl.ANY)],
        out_specs=pl.BlockSpec(memory_space=pl.ANY),
        out_shape=jax.ShapeDtypeStruct((m, d), F32),
        scratch_shapes=[pltpu.VMEM((tm, d), F32)],
        compiler_params=pltpu.CompilerParams(dimension_semantics=("parallel", "arbitrary"),
                                             vmem_limit_bytes=PEER_DENSE_VMEM_LIMIT),
        name="peer_dense",
    )(xn, u_tab, v_tab, gates, h)


def _row_tile(m, cap):
    return min(m, cap)


def _trunk(x, attend, recur, w, lam_init):
    m, d = x.shape
    att_w = w["q_norm_cols"]
    xn = rmsnorm_bf16(x, w["norm1_w"], _row_tile(m, 256))
    z = matmul(xn, w["w_in"], tm=_row_tile(m, 1024), tn=512)
    qn, kn, knb, v, vb = qkv_post(z, w["q_norm_w"], w["k_norm_w"], att_w, _row_tile(m, 256))
    att = attend(qn, kn, knb, v, vb)
    rnn, s_fin = recur(z)
    mix = jnp.concatenate([att, rnn], axis=-1)
    h = matmul(mix, w["w_out"], x, tm=_row_tile(m, 1024), tn=512)
    hn = rmsnorm_bf16(h, w["norm2_w"], _row_tile(m, 256))
    pq = matmul(hn, w["peer_wq"], tm=_row_tile(m, 1024), tn=512)
    if "peer_tables_bf16" not in w:
        a, b, g, ub, vb16 = peer_topk(pq, w["peer_subkeys"], 128, tables=(w["peer_u"], w["peer_v"]))
        w["peer_tables_bf16"] = (ub, vb16)
    else:
        a, b, g = peer_topk(pq, w["peer_subkeys"], 128)
    gates = peer_gates(a.T, b.T, g.T, 64)
    y = peer_dense(hn, *w["peer_tables_bf16"], gates, h, _row_tile(m, 1024), 512)
    return y, kn, v, s_fin


def kernel(x_prompt, x_sample, cache_k, cache_v, state_rnn, page_table, norm1_w, w_in, q_norm_w, k_norm_w, lambda_q1, lambda_k1, lambda_q2, lambda_k2, sub_norm_w, lb_param, rnn_norm_w, w_out, norm2_w, peer_wq, peer_subkeys, peer_u, peer_v):
    depth = w_in.shape[0]
    assert depth == 1, "single-layer trunk"
    layer = 0
    batch, seq, d = x_prompt.shape
    n_dec, dec_seq, _ = x_sample.shape
    assert dec_seq == 1
    h_att = cache_k.shape[3]
    att_w = h_att * VAL_DIM
    h_rnn = state_rnn.shape[2]
    lam_init = 0.8 - 0.6 * math.exp(-0.3 * layer)
    rnn_col0 = 3 * att_w // LANES

    w = dict(norm1_w=norm1_w[layer], w_in=w_in[layer], q_norm_w=q_norm_w[layer], k_norm_w=k_norm_w[layer],
             w_out=w_out[layer], norm2_w=norm2_w[layer], peer_wq=peer_wq[layer],
             peer_subkeys=peer_subkeys[layer], peer_u=peer_u[layer], peer_v=peer_v[layer], q_norm_cols=att_w)
    lams = [p[layer].reshape(1, HEAD_DIM) for p in (lambda_q1, lambda_k1, lambda_q2, lambda_k2)]
    sub_w = sub_norm_w[layer]
    rnn_w = rnn_norm_w[layer]

    s0 = jnp.zeros((batch, h_rnn, RNN_DK, RNN_DV), F32)
    y_p, k_p, v_p, s_p = _trunk(
        x_prompt.reshape(batch * seq, d),
        lambda qn, kn, knb, v, vb: attn_prompt(qn, knb, vb, lams, sub_w, batch, seq, lam_init, ATTN_Q_BLOCK),
        lambda z: hgrn_prompt(z, lb_param, s0, rnn_w, batch, seq, rnn_col0),
        w, lam_init)

    m_pad = LANES
    xs = jnp.zeros((m_pad, d), F32).at[:n_dec].set(x_sample.reshape(n_dec, d))
    ck = cache_k[layer].reshape(cache_k.shape[1], PAGE_SIZE * h_att * 2, HEAD_DIM)
    cv = cache_v[layer]

    def attend_s(qn, kn, knb, v, vb):
        split = lambda a: a[:n_dec].astype(F32).reshape(n_dec, h_att, 2, HEAD_DIM).transpose(0, 2, 1, 3)
        o = attn_sample(split(qn), split(kn), v[:n_dec].reshape(n_dec, h_att, VAL_DIM), ck, cv, page_table,
                        lams, sub_w, lam_init, SAMPLE_PAGES_PER_STEP)
        return jnp.zeros((m_pad, att_w), BF16).at[:n_dec].set(o.reshape(n_dec, att_w))

    def recur_s(z):
        o, s_new = hgrn_sample(z, lb_param, state_rnn[layer], rnn_w, rnn_col0)
        return jnp.zeros((m_pad, o.shape[1]), BF16).at[:n_dec].set(o), s_new

    y_s, k_s, v_s, s_s = _trunk(xs, attend_s, recur_s, w, lam_init)

    return (y_p.reshape(batch, seq, d),
            y_s[:n_dec].reshape(n_dec, 1, d),
            k_p.reshape(1, batch, seq, h_att, 2, HEAD_DIM),
            v_p.reshape(1, batch, seq, h_att, VAL_DIM),
            k_s[:n_dec].reshape(1, n_dec, 1, h_att, 2, HEAD_DIM),
            v_s[:n_dec].reshape(1, n_dec, 1, h_att, VAL_DIM),
            s_p[None].astype(state_rnn.dtype),
            s_s[None].astype(state_rnn.dtype))
```

```python
import functools
import math

import jax
import jax.numpy as jnp
from jax import lax
from jax.experimental import pallas as pl
from jax.experimental.pallas import tpu as pltpu

F32 = jnp.float32
BF16 = jnp.bfloat16

LANES = 128
LOG2_LANES = 7
HEAD_DIM = 128
VAL_DIM = 2 * HEAD_DIM
RNN_DK = 128
RNN_DV = 128
HGRN_CHUNK = 128
NORM_EPS = 1e-6
ATTN_SCALE = HEAD_DIM ** -0.5
LOG2_E = math.log2(math.e)
N_KEYS = 128
PEER_TOPK = 16
PAGE_SIZE = 128
GATE_ROWS = 8
SAMPLE_PAGES_PER_STEP = 8
ATTN_HEADS_PER_STEP = 4
ATTN_Q_BLOCK = 256
ATTN_KV_BLOCK = 1024
VMEM_LIMIT = 56 * 1024 * 1024
PEER_DENSE_VMEM_LIMIT = 60 * 1024 * 1024

NT_DIMS = (((1,), (1,)), ((), ()))
TN_DIMS = (((0,), (0,)), ((), ()))


def _params(*sem):
    return pltpu.CompilerParams(dimension_semantics=sem, vmem_limit_bytes=VMEM_LIMIT)


def _sigmoid(x):
    return 1.0 / (1.0 + jnp.exp(-x))


def _rmsnorm_kernel(x_ref, w_ref, o_ref):
    x = x_ref[...]
    ms = jnp.mean(x * x, axis=-1, keepdims=True)
    o_ref[...] = (x * lax.rsqrt(ms + NORM_EPS) * w_ref[...]).astype(o_ref.dtype)


def rmsnorm_bf16(x, w, tm):
    m, d = x.shape
    return pl.pallas_call(
        _rmsnorm_kernel,
        grid=(m // tm,),
        in_specs=[pl.BlockSpec((tm, d), lambda i: (i, 0)),
                  pl.BlockSpec((1, d), lambda i: (0, 0))],
        out_specs=pl.BlockSpec((tm, d), lambda i: (i, 0)),
        out_shape=jax.ShapeDtypeStruct((m, d), BF16),
        compiler_params=_params("parallel"),
        name="rmsnorm",
    )(x, w.reshape(1, d))


def _matmul_kernel(*refs, n_x, has_res):
    x_refs, w_ref = refs[:n_x], refs[n_x]
    r_ref = refs[n_x + 1] if has_res else None
    o_ref, wb_ref = refs[-2:]

    @pl.when(pl.program_id(1) == 0)
    def _():
        wb_ref[...] = w_ref[...].astype(BF16)

    acc, k0 = None, 0
    for x_ref in x_refs:
        part = jnp.dot(x_ref[...], wb_ref[k0:k0 + x_ref.shape[1], :], preferred_element_type=F32)
        acc = part if acc is None else acc + part
        k0 += x_ref.shape[1]
    if has_res:
        acc = r_ref[...] + acc
    o_ref[...] = acc


def matmul(xs, w, res=None, *, tm, tn):
    xs = xs if isinstance(xs, tuple) else (xs,)
    m = xs[0].shape[0]
    k, n = w.shape
    assert sum(x.shape[1] for x in xs) == k
    in_specs = [pl.BlockSpec((tm, x.shape[1]), lambda j, i: (i, 0)) for x in xs]
    in_specs.append(pl.BlockSpec((k, tn), lambda j, i: (0, j)))
    args = [*xs, w]
    if res is not None:
        in_specs.append(pl.BlockSpec((tm, tn), lambda j, i: (i, j)))
        args.append(res)
    return pl.pallas_call(
        functools.partial(_matmul_kernel, n_x=len(xs), has_res=res is not None),
        grid=(n // tn, m // tm),
        in_specs=in_specs,
        out_specs=pl.BlockSpec((tm, tn), lambda j, i: (i, j)),
        out_shape=jax.ShapeDtypeStruct((m, n), F32),
        scratch_shapes=[pltpu.VMEM((k, tn), BF16)],
        compiler_params=_params("arbitrary", "arbitrary"),
        name="matmul",
    )(*args)


def _qkv_kernel(zq_ref, zk_ref, zv_ref, qw_ref, kw_ref, qn_ref, kn_ref, knb_ref, vb_ref):
    tm, width = zq_ref.shape
    groups = width // HEAD_DIM
    for g in range(groups):
        sl = slice(g * HEAD_DIM, (g + 1) * HEAD_DIM)
        xq = zq_ref[:, sl]
        yq = xq * lax.rsqrt(jnp.mean(xq * xq, axis=-1, keepdims=True) + NORM_EPS) * qw_ref[...]
        qn_ref[:, sl] = yq.astype(BF16)
        xk = zk_ref[:, sl]
        yk = xk * lax.rsqrt(jnp.mean(xk * xk, axis=-1, keepdims=True) + NORM_EPS) * kw_ref[...]
        kn_ref[pl.ds(g, tm, stride=groups), :] = yk
        knb_ref[:, sl] = yk.astype(BF16)
    vb_ref[...] = zv_ref[...].astype(BF16)


def qkv_post(z, q_norm_w, k_norm_w, width, tm):
    m = z.shape[0]
    groups = width // HEAD_DIM
    blk = lambda c: pl.BlockSpec((tm, width), lambda i, c=c: (i, c))
    wspec = pl.BlockSpec((1, HEAD_DIM), lambda i: (0, 0))
    out = lambda dt: jax.ShapeDtypeStruct((m, width), dt)
    return pl.pallas_call(
        _qkv_kernel,
        grid=(m // tm,),
        in_specs=[blk(0), blk(1), blk(2), wspec, wspec],
        out_specs=[blk(0), pl.BlockSpec((tm * groups, HEAD_DIM), lambda i: (i, 0)), blk(0), blk(0)],
        out_shape=[out(BF16), jax.ShapeDtypeStruct((m * groups, HEAD_DIM), F32), out(BF16), out(BF16)],
        compiler_params=_params("parallel"),
        name="qkv_post",
    )(z, z, z, q_norm_w.reshape(1, HEAD_DIM), k_norm_w.reshape(1, HEAD_DIM))


def _lambda(lq1_ref, lk1_ref, lq2_ref, lk2_ref, lam_init):
    a = jnp.sum(lq1_ref[...] * lk1_ref[...], axis=-1, keepdims=True)
    b = jnp.sum(lq2_ref[...] * lk2_ref[...], axis=-1, keepdims=True)
    return jnp.exp(a) - jnp.exp(b) + lam_init


def _sub_norm(o, w, lam_init):
    ms = jnp.mean(o * o, axis=-1, keepdims=True)
    return o * lax.rsqrt(ms + NORM_EPS) * w * (1.0 - lam_init)


def _attn_prompt_kernel(q_ref, k_ref, v_ref, lq1_ref, lk1_ref, lq2_ref, lk2_ref, sw_ref, o_ref,
                        m_ref, l_ref, acc_ref, *, tq, lam_init):
    qi = pl.program_id(2)
    m_ref[...] = jnp.full(m_ref.shape, -jnp.inf, F32)
    l_ref[...] = jnp.zeros(l_ref.shape, F32)
    acc_ref[...] = jnp.zeros(acc_ref.shape, F32)
    tk = ATTN_KV_BLOCK
    n_full = (qi * tq + 1) // tk

    n_chain = m_ref.shape[0]

    def block(j, masked):
        kv_rows = pl.ds(pl.multiple_of(j * tk, tk), tk)
        for ch in range(n_chain):
            sl = slice(ch * HEAD_DIM, (ch + 1) * HEAD_DIM)
            vsl = slice(ch // 2 * VAL_DIM, (ch // 2 + 1) * VAL_DIM)
            s = lax.dot_general(q_ref[:, sl], k_ref[kv_rows, sl], NT_DIMS,
                                preferred_element_type=F32) * (ATTN_SCALE * LOG2_E)
            if masked:
                row = qi * tq + lax.broadcasted_iota(jnp.int32, (tq, tk), 0)
                col = j * tk + lax.broadcasted_iota(jnp.int32, (tq, tk), 1)
                s = jnp.where(col <= row, s, -jnp.inf)
            m_old = m_ref[ch]
            m_new = jnp.maximum(m_old, jnp.max(s, axis=-1, keepdims=True))
            alpha = jnp.exp2(m_old - m_new)
            p = jnp.exp2(s - m_new)
            p_lanes = sum(p[:, t * LANES:(t + 1) * LANES] for t in range(1, tk // LANES)) + p[:, :LANES]
            l_ref[ch] = alpha * l_ref[ch] + p_lanes
            acc_ref[ch] = alpha * acc_ref[ch] + jnp.dot(p.astype(BF16), v_ref[kv_rows, vsl],
                                                        preferred_element_type=F32)
            m_ref[ch] = m_new

    def full_block(j, carry):
        block(j, masked=False)
        return carry

    lax.fori_loop(0, n_full, full_block, 0)
    block(n_full, masked=True)
    lam = _lambda(lq1_ref, lk1_ref, lq2_ref, lk2_ref, lam_init)
    for hh in range(n_chain // 2):
        l0 = jnp.sum(l_ref[2 * hh], axis=-1, keepdims=True)
        l1 = jnp.sum(l_ref[2 * hh + 1], axis=-1, keepdims=True)
        o = acc_ref[2 * hh] / l0 - lam * (acc_ref[2 * hh + 1] / l1)
        o_ref[:, hh * VAL_DIM:(hh + 1) * VAL_DIM] = _sub_norm(o, sw_ref[...], lam_init).astype(o_ref.dtype)


def attn_prompt(qn, knb, vb, lams, sub_norm_w, batch, seq, lam_init, tq):
    m, width = qn.shape
    assert ATTN_KV_BLOCK % tq == 0 and seq % ATTN_KV_BLOCK == 0
    hp = ATTN_HEADS_PER_STEP
    blk_w = hp * VAL_DIM
    nq = seq // tq
    vec = pl.BlockSpec((1, HEAD_DIM), lambda b, h, i: (0, 0))
    return pl.pallas_call(
        functools.partial(_attn_prompt_kernel, tq=tq, lam_init=lam_init),
        grid=(batch, width // blk_w, nq),
        in_specs=[pl.BlockSpec((tq, blk_w), lambda b, h, i: (b * nq + i, h)),
                  pl.BlockSpec((seq, blk_w), lambda b, h, i: (b, h)),
                  pl.BlockSpec((seq, blk_w), lambda b, h, i: (b, h)),
                  vec, vec, vec, vec,
                  pl.BlockSpec((1, VAL_DIM), lambda b, h, i: (0, 0))],
        out_specs=pl.BlockSpec((tq, blk_w), lambda b, h, i: (b * nq + i, h)),
        out_shape=jax.ShapeDtypeStruct((m, width), BF16),
        scratch_shapes=[pltpu.VMEM((2 * hp, tq, 1), F32), pltpu.VMEM((2 * hp, tq, LANES), F32),
                        pltpu.VMEM((2 * hp, tq, VAL_DIM), F32)],
        compiler_params=_params("parallel", "parallel", "arbitrary"),
        name="attn_prompt",
    )(qn, knb, vb, *lams, sub_norm_w.reshape(1, VAL_DIM))


def _attn_sample_kernel(pt_ref, q_ref, kn_ref, vn_ref, lq1_ref, lk1_ref, lq2_ref, lk2_ref, sw_ref, *rest,
                        n_grp, lam_init):
    del pt_ref
    k_refs, v_refs = rest[:n_grp], rest[n_grp:2 * n_grp]
    o_ref, m_ref, l_ref, acc_ref = rest[2 * n_grp:]
    step = pl.program_id(1)
    heads = q_ref.shape[1]

    @pl.when(step == 0)
    def _():
        m_ref[...] = jnp.full(m_ref.shape, -jnp.inf, F32)
        l_ref[...] = jnp.zeros(l_ref.shape, F32)
        acc_ref[...] = jnp.zeros(acc_ref.shape, F32)

    q = [q_ref[c] * (ATTN_SCALE * LOG2_E) for c in range(2)]
    rows = PAGE_SIZE * heads
    r_blk = jnp.right_shift(lax.broadcasted_iota(jnp.int32, (VAL_DIM, VAL_DIM), 0), LOG2_LANES)
    c_blk = jnp.right_shift(lax.broadcasted_iota(jnp.int32, (VAL_DIM, VAL_DIM), 1), LOG2_LANES)
    ones_blk = jnp.where(r_blk == c_blk, 1.0, 0.0).astype(BF16)
    scores = []
    for g in range(n_grp):
        prod = [(k_refs[g][pl.ds(c, rows, stride=2), :].reshape(PAGE_SIZE, heads, HEAD_DIM) * q[c][None])
                .reshape(rows, HEAD_DIM) for c in range(2)]
        lhs = jnp.concatenate(prod, axis=-1).astype(BF16)
        scores.append(jnp.dot(lhs, ones_blk, preferred_element_type=F32).reshape(PAGE_SIZE, heads, VAL_DIM))
    for c in range(2):
        half = slice(c * HEAD_DIM, (c + 1) * HEAD_DIM)
        m_old = m_ref[c]
        m_new = m_old
        for s in scores:
            m_new = jnp.maximum(m_new, jnp.max(s[:, :, half], axis=0))
        alpha = jnp.exp2(m_old - m_new)
        l_new = alpha * l_ref[c]
        acc = jnp.concatenate([alpha, alpha], axis=-1) * acc_ref[c]
        for g in range(n_grp):
            p = jnp.exp2(scores[g][:, :, half] - m_new[None])
            l_new = l_new + jnp.sum(p, axis=0)
            acc = acc + jnp.sum(jnp.concatenate([p, p], axis=-1) * v_refs[g][...], axis=0)
        m_ref[c] = m_new
        l_ref[c] = l_new
        acc_ref[c] = acc

    @pl.when(step == pl.num_programs(1) - 1)
    def _():
        outs = []
        for c in range(2):
            s_new = jnp.sum(kn_ref[c] * q[c], axis=-1, keepdims=True)
            m_old = m_ref[c]
            m_new = jnp.maximum(m_old, s_new)
            alpha = jnp.exp2(m_old - m_new)
            p = jnp.exp2(s_new - m_new)
            wide = lambda a: jnp.concatenate([a, a], axis=-1)
            outs.append((wide(alpha) * acc_ref[c] + wide(p) * vn_ref[...]) / wide(alpha * l_ref[c] + p))
        lam = _lambda(lq1_ref, lk1_ref, lq2_ref, lk2_ref, lam_init)
        o_ref[...] = _sub_norm(outs[0] - lam * outs[1], sw_ref[...], lam_init).astype(o_ref.dtype)


def attn_sample(q, kn, vn, cache_k, cache_v, page_table, lams, sub_norm_w, lam_init, n_grp):
    bsz, _, heads, _ = q.shape
    n_pages = page_table.shape[1]
    qspec = pl.BlockSpec((None, 2, heads, HEAD_DIM), lambda b, s, pt: (b, 0, 0, 0))
    vspec = pl.BlockSpec((None, heads, VAL_DIM), lambda b, s, pt: (b, 0, 0))
    vec = pl.BlockSpec((1, HEAD_DIM), lambda b, s, pt: (0, 0))

    def page(shape, g):
        zeros = (0,) * (len(shape) - 1)
        return pl.BlockSpec((None,) + tuple(shape[1:]),
                            lambda b, s, pt: (pt[b * n_pages + s * n_grp + g],) + zeros)

    grid_spec = pltpu.PrefetchScalarGridSpec(
        num_scalar_prefetch=1,
        grid=(bsz, n_pages // n_grp),
        in_specs=[qspec, qspec, vspec, vec, vec, vec, vec, pl.BlockSpec((1, VAL_DIM), lambda b, s, pt: (0, 0))]
        + [page(cache_k.shape, g) for g in range(n_grp)] + [page(cache_v.shape, g) for g in range(n_grp)],
        out_specs=vspec,
        scratch_shapes=[pltpu.VMEM((2, heads, HEAD_DIM), F32), pltpu.VMEM((2, heads, HEAD_DIM), F32),
                        pltpu.VMEM((2, heads, VAL_DIM), F32)],
    )
    return pl.pallas_call(
        functools.partial(_attn_sample_kernel, n_grp=n_grp, lam_init=lam_init),
        grid_spec=grid_spec,
        out_shape=jax.ShapeDtypeStruct((bsz, heads, VAL_DIM), BF16),
        compiler_params=_params("parallel", "arbitrary"),
        name="attn_sample",
    )(page_table.reshape(-1), q, kn, vn, *lams, sub_norm_w.reshape(1, VAL_DIM),
      *([cache_k] * n_grp), *([cache_v] * n_grp))


def _lower_bound(lb_ref):
    p = lb_ref[...]
    e = jnp.exp(p - jnp.max(p, axis=0, keepdims=True))
    return e[0:1, :] / jnp.sum(e, axis=0, keepdims=True)


def _split3(x):
    hi = x.astype(BF16)
    r = x - hi.astype(F32)
    mid = r.astype(BF16)
    lo = (r - mid.astype(F32)).astype(BF16)
    return hi, mid, lo


def _hgrn_prompt_kernel(zq_ref, zf_ref, zi_ref, zg_ref, lb_ref, s0_ref, nw_ref, o_ref, sfin_ref,
                        q_s, k_s, b_s, f_s, oi_s, st_s, dec_s, *, seq):
    c = HGRN_CHUNK
    n_chunks = seq // c
    lb = _lower_bound(lb_ref)
    zf = zf_ref[...]
    sig = _sigmoid(zf)
    f = lb + (1.0 - lb) * sig
    f_s[...] = f
    k_s[...] = (1.0 - lb) * (1.0 - sig)
    zq = zq_ref[...]
    q_s[...] = zq * _sigmoid(zq)

    grp = 256
    r = lax.broadcasted_iota(jnp.int32, (grp, grp), 0)
    cc = lax.broadcasted_iota(jnp.int32, (grp, grp), 1)
    shift = c.bit_length() - 1
    same_chunk = jnp.right_shift(r, shift) == jnp.right_shift(cc, shift)
    tri = jnp.where(same_chunk & (cc <= r), 1.0, 0.0).astype(BF16)
    b_s[...] = jnp.log2(f)
    for gi in range(seq // grp):
        rows = slice(gi * grp, (gi + 1) * grp)
        hi, mid, lo = _split3(b_s[rows, :])
        b_s[rows, :] = (jnp.dot(tri, hi, preferred_element_type=F32)
                        + jnp.dot(tri, mid, preferred_element_type=F32)
                        + jnp.dot(tri, lo, preferred_element_type=F32))

    trow = lax.broadcasted_iota(jnp.int32, (c, RNN_DK), 0)
    arow = lax.broadcasted_iota(jnp.int32, (c, c), 0)
    acol = lax.broadcasted_iota(jnp.int32, (c, c), 1)
    sublane = 8

    def phase1(n, carry):
        base = pl.multiple_of(n * c, c)
        rows = pl.ds(base, c)
        b = b_s[rows, :]
        q = q_s[rows, :]
        k = k_s[rows, :]
        v = zi_ref[rows, :]
        b_last = b[c - 1:c, :]

        def b_row(r, nrows):
            return jnp.broadcast_to(b_s[pl.ds(base + r, 1), :], (nrows, RNN_DK))

        att = jnp.where(arow == acol, jnp.sum(q * k, axis=-1, keepdims=True), 0.0)
        m = 1
        while m < c:
            blk = 2 * m
            if m == 1:
                odd = (trow & 1) != 0
                qh = jnp.where(odd, q * f_s[rows, :], 0.0)
                kh = jnp.where(odd, 0.0, k)
            elif m < sublane:
                cands = [jnp.concatenate([b_row(t0 + i * blk + m - 1, sublane) for t0 in range(0, c, sublane)], axis=0)
                         for i in range(sublane // blk)]
                ref = cands[-1]
                for i in range(sublane // blk - 2, -1, -1):
                    ref = jnp.where((trow & (sublane - 1)) < (i + 1) * blk, cands[i], ref)
                e = b - ref
                x = jnp.exp2(jnp.minimum(e, -e))
                upper = (trow & m) != 0
                qh = jnp.where(upper, q * x, 0.0)
                kh = jnp.where(upper, 0.0, k * x)
            else:
                zeros = jnp.zeros((m, RNN_DK), F32)
                q_parts, k_parts = [], []
                for s0 in range(0, c, blk):
                    lo, up = slice(s0, s0 + m), slice(s0 + m, s0 + blk)
                    ref = b_row(s0 + m - 1, m)
                    q_parts += [zeros, q[up] * jnp.exp2(b[up] - ref)]
                    k_parts += [k[lo] * jnp.exp2(ref - b[lo]), zeros]
                qh = jnp.concatenate(q_parts, axis=0)
                kh = jnp.concatenate(k_parts, axis=0)
            a = lax.dot_general(qh.astype(BF16), kh.astype(BF16), NT_DIMS, preferred_element_type=F32)
            if blk < c:
                shift_blk = blk.bit_length() - 1
                a = jnp.where(jnp.right_shift(arow, shift_blk) == jnp.right_shift(acol, shift_blk), a, 0.0)
            att = att + a
            m = blk
        oi_s[rows, :] = jnp.dot(att.astype(BF16), v.astype(BF16), preferred_element_type=F32)
        q_s[rows, :] = q * jnp.exp2(b)
        kt = k * jnp.exp2(b_last - b)
        st_s[n] = lax.dot_general(v.astype(BF16), kt.astype(BF16), TN_DIMS, preferred_element_type=F32)
        dec_s[pl.ds(n, 1), :] = jnp.exp2(b_last)
        return carry

    lax.fori_loop(0, n_chunks, phase1, 0, unroll=4)

    def phase2(n, st):
        inc = st_s[n]
        st_s[n] = st
        return st * dec_s[pl.ds(n, 1), :] + inc

    st_fin = lax.fori_loop(0, n_chunks, phase2, s0_ref[...].T)
    sfin_ref[...] = st_fin.T

    def phase3(n, carry):
        rows = pl.ds(pl.multiple_of(n * c, c), c)
        oi_s[rows, :] += lax.dot_general(q_s[rows, :].astype(BF16), st_s[n].astype(BF16), NT_DIMS,
                                         preferred_element_type=F32)
        return carry

    lax.fori_loop(0, n_chunks, phase3, 0, unroll=4)
    o = oi_s[...]
    ms = jnp.mean(o * o, axis=-1, keepdims=True)
    zg = zg_ref[...]
    o_ref[...] = (o * lax.rsqrt(ms + NORM_EPS) * nw_ref[...] * (zg * _sigmoid(zg))).astype(o_ref.dtype)


def hgrn_prompt(z, lb_param, s0, rnn_norm_w, batch, seq, col0):
    m = z.shape[0]
    heads = s0.shape[1]
    zblk = lambda g: pl.BlockSpec((seq, RNN_DK), lambda b, h, g=g: (b, col0 + g * heads + h))
    n_chunks = seq // HGRN_CHUNK
    return pl.pallas_call(
        functools.partial(_hgrn_prompt_kernel, seq=seq),
        grid=(batch, heads),
        in_specs=[zblk(0), zblk(1), zblk(2), zblk(3),
                  pl.BlockSpec((lb_param.shape[0], RNN_DK), lambda b, h: (0, h)),
                  pl.BlockSpec((None, None, RNN_DK, RNN_DV), lambda b, h: (b, h, 0, 0)),
                  pl.BlockSpec((1, RNN_DV), lambda b, h: (0, 0))],
        out_specs=[pl.BlockSpec((seq, RNN_DV), lambda b, h: (b, h)),
                   pl.BlockSpec((None, None, RNN_DK, RNN_DV), lambda b, h: (b, h, 0, 0))],
        out_shape=[jax.ShapeDtypeStruct((m, heads * RNN_DV), BF16),
                   jax.ShapeDtypeStruct(s0.shape, F32)],
        scratch_shapes=[pltpu.VMEM((seq, RNN_DK), F32), pltpu.VMEM((seq, RNN_DK), F32),
                        pltpu.VMEM((seq, RNN_DK), F32), pltpu.VMEM((seq, RNN_DK), F32),
                        pltpu.VMEM((seq, RNN_DV), F32),
                        pltpu.VMEM((n_chunks, RNN_DV, RNN_DK), F32),
                        pltpu.VMEM((n_chunks, RNN_DK), F32)],
        compiler_params=_params("parallel", "parallel"),
        name="hgrn_prompt",
    )(z, z, z, z, lb_param, s0, rnn_norm_w.reshape(1, RNN_DV))


def _hgrn_sample_kernel(zq_ref, zf_ref, zi_ref, zg_ref, lb_ref, s0_ref, nw_ref, o_ref, s_ref):
    bsz = s0_ref.shape[0]
    lb = _lower_bound(lb_ref)
    zf = zf_ref[...]
    f_t = (lb + (1.0 - lb) * _sigmoid(zf)).T
    k_t = ((1.0 - lb) * _sigmoid(-zf)).T
    zq = zq_ref[...]
    q_t = (zq * _sigmoid(zq)).T
    for b in range(bsz):
        v = zi_ref[b:b + 1, :]
        s_new = f_t[:, b:b + 1] * s0_ref[b] + k_t[:, b:b + 1] * v
        s_ref[b] = s_new
        o = jnp.sum(q_t[:, b:b + 1] * s_new, axis=0, keepdims=True)
        ms = jnp.mean(o * o, axis=-1, keepdims=True)
        zg = zg_ref[b:b + 1, :]
        o_ref[b:b + 1, :] = (o * lax.rsqrt(ms + NORM_EPS) * nw_ref[...] * (zg * _sigmoid(zg))).astype(o_ref.dtype)


def hgrn_sample(z, lb_param, s0, rnn_norm_w, col0):
    bsz, heads = s0.shape[:2]
    zblk = lambda g: pl.BlockSpec((bsz, RNN_DK), lambda h, g=g: (0, col0 + g * heads + h))
    return pl.pallas_call(
        _hgrn_sample_kernel,
        grid=(heads,),
        in_specs=[zblk(0), zblk(1), zblk(2), zblk(3),
                  pl.BlockSpec((lb_param.shape[0], RNN_DK), lambda h: (0, h)),
                  pl.BlockSpec((bsz, None, RNN_DK, RNN_DV), lambda h: (0, h, 0, 0)),
                  pl.BlockSpec((1, RNN_DV), lambda h: (0, 0))],
        out_specs=[pl.BlockSpec((bsz, RNN_DV), lambda h: (0, h)),
                   pl.BlockSpec((bsz, None, RNN_DK, RNN_DV), lambda h: (0, h, 0, 0))],
        out_shape=[jax.ShapeDtypeStruct((bsz, heads * RNN_DV), BF16),
                   jax.ShapeDtypeStruct(s0.shape, F32)],
        compiler_params=_params("parallel"),
        name="hgrn_sample",
    )(z, z, z, z, lb_param, s0, rnn_norm_w.reshape(1, RNN_DV))


def _top16(s, ids):
    big = jnp.int32(2 ** 30)
    vals, sel = [], []
    for _ in range(PEER_TOPK):
        m = jnp.max(s, axis=0, keepdims=True)
        pick = jnp.min(jnp.where(s == m, ids, big), axis=0, keepdims=True)
        s = jnp.where(ids == pick, -jnp.inf, s)
        vals.append(m)
        sel.append(pick)
    return jnp.concatenate(vals, axis=0), jnp.concatenate(sel, axis=0)


def _peer_topk_kernel(*refs, cast_tables):
    if cast_tables:
        q_ref, sk_ref, u_ref, v_ref, a_ref, b_ref, g_ref, ub_ref, vb_ref = refs
        ub_ref[...] = u_ref[...].astype(BF16)
        vb_ref[...] = v_ref[...].astype(BF16)
    else:
        q_ref, sk_ref, a_ref, b_ref, g_ref = refs
    tm = q_ref.shape[0]
    heads = sk_ref.shape[0]
    key_ids = lax.broadcasted_iota(jnp.int32, (N_KEYS, tm), 0)
    k = PEER_TOPK
    for h in range(heads):
        half = []
        for c in range(2):
            col = (2 * h + c) * HEAD_DIM
            sc = lax.dot_general(sk_ref[h, c].astype(BF16), q_ref[:, col:col + HEAD_DIM].astype(BF16),
                                 NT_DIMS, preferred_element_type=F32)
            half.append(_top16(sc, key_ids))
        (s1, i1), (s2, i2) = half
        cand, eid, fid = [], [], []
        crow = lax.broadcasted_iota(jnp.int32, (k, tm), 0)
        for r in range(k // 2):
            nc = k if r == 0 else k // 2
            cand.append(s1[r:r + 1, :] + s2[:nc, :])
            eid.append(i1[r:r + 1, :] * N_KEYS + i2[:nc, :])
            fid.append(r * k + crow[:nc, :])
        cand.append(s1[k // 2:, :] + s2[0:1, :])
        eid.append(i1[k // 2:, :] * N_KEYS + i2[0:1, :])
        fid.append((k // 2 + crow[:k // 2, :]) * k)
        cand = jnp.concatenate(cand, axis=0)
        eid = jnp.concatenate(eid, axis=0)
        fid = jnp.concatenate(fid, axis=0)
        best, pick = _top16(cand, fid)
        e_sel = []
        for j in range(k):
            e_sel.append(jnp.max(jnp.where(fid == pick[j:j + 1, :], eid, -1), axis=0, keepdims=True))
        e_sel = jnp.concatenate(e_sel, axis=0)
        p = jnp.exp(best - best[0:1, :])
        gate = p / jnp.sum(p, axis=0, keepdims=True)
        rows = slice(h * k, (h + 1) * k)
        a_ref[rows, :] = jnp.right_shift(e_sel, LOG2_LANES)
        b_ref[rows, :] = jnp.bitwise_and(e_sel, N_KEYS - 1)
        g_ref[rows, :] = gate


def peer_topk(q, subkeys, tm, tables=None):
    m = q.shape[0]
    heads = subkeys.shape[0]
    n_sel = heads * PEER_TOPK
    steps = m // tm
    out = pl.BlockSpec((n_sel, tm), lambda i: (0, i))
    in_specs = [pl.BlockSpec((tm, q.shape[1]), lambda i: (i, 0)),
                pl.BlockSpec(subkeys.shape, lambda i: (0, 0, 0, 0))]
    out_specs = [out, out, out]
    out_shape = [jax.ShapeDtypeStruct((n_sel, m), jnp.int32),
                 jax.ShapeDtypeStruct((n_sel, m), jnp.int32),
                 jax.ShapeDtypeStruct((n_sel, m), F32)]
    args = [q, subkeys]
    if tables is not None:
        n_exp, d = tables[0].shape
        slab = pl.BlockSpec((n_exp // steps, d), lambda i: (i, 0))
        in_specs += [slab, slab]
        out_specs += [slab, slab]
        out_shape += [jax.ShapeDtypeStruct((n_exp, d), BF16)] * 2
        args += list(tables)
    return pl.pallas_call(
        functools.partial(_peer_topk_kernel, cast_tables=tables is not None),
        grid=(steps,),
        in_specs=in_specs,
        out_specs=out_specs,
        out_shape=out_shape,
        compiler_params=_params("parallel"),
        name="peer_topk",
    )(*args)


def _peer_gates_kernel(a_ref, b_ref, g_ref, o_ref):
    tg, n_sel = a_ref.shape
    ids = lax.broadcasted_iota(jnp.int32, (N_KEYS, n_sel), 0)

    def body(t, carry):
        row = pl.ds(t, 1)
        left = jnp.where(ids == a_ref[row, :], 1.0, 0.0).astype(BF16)
        right = jnp.where(ids == b_ref[row, :], g_ref[row, :], 0.0).astype(BF16)
        o_ref[t] = lax.dot_general(left, right, NT_DIMS, preferred_element_type=F32).astype(o_ref.dtype)
        return carry

    lax.fori_loop(0, tg, body, 0, unroll=16)


def peer_gates(a, b, g, tg):
    m, n_sel = a.shape
    spec = pl.BlockSpec((tg, n_sel), lambda i: (i, 0))
    return pl.pallas_call(
        _peer_gates_kernel,
        grid=(m // tg,),
        in_specs=[spec, spec, spec],
        out_specs=pl.BlockSpec((tg, N_KEYS, N_KEYS), lambda i: (i, 0, 0)),
        out_shape=jax.ShapeDtypeStruct((m, N_KEYS, N_KEYS), F32),
        compiler_params=_params("parallel"),
        name="peer_gates",
    )(a, b, g)


def _gelu(x):
    return 0.5 * x * (1.0 + lax.erf(x * (2.0 ** -0.5)))


def _peer_dense_kernel(x_ref, u_ref, v_ref, gs_ref, h_hbm, y_hbm, y_ref, *, tm):
    tile = pl.ds(pl.program_id(0) * tm, tm)

    @pl.when(pl.program_id(1) == 0)
    def _():
        pltpu.sync_copy(h_hbm.at[tile, :], y_ref)

    hid = lax.dot_general(x_ref[...], u_ref[...], NT_DIMS, preferred_element_type=F32)
    n_sub = hid.shape[1] // N_KEYS
    first = (pl.program_id(1) % (gs_ref.shape[1] // n_sub)) * n_sub
    gs2 = gs_ref.reshape(tm * GATE_ROWS, N_KEYS)
    w = []
    for j in range(n_sub):
        sl = slice(j * N_KEYS, (j + 1) * N_KEYS)
        gate = gs2[pl.ds(first + j, tm, stride=GATE_ROWS), :]
        w.append((gate * _gelu(hid[:, sl])).astype(BF16))
    w = jnp.concatenate(w, axis=-1)
    y_ref[...] += jnp.dot(w, v_ref[...], preferred_element_type=F32)

    @pl.when(pl.program_id(1) == pl.num_programs(1) - 1)
    def _():
        pltpu.sync_copy(y_ref, y_hbm.at[tile, :])


def peer_dense(xn, u_tab, v_tab, gates, h, tm, te):
    m, d = xn.shape
    n_exp = u_tab.shape[0]
    return pl.pallas_call(
        functools.partial(_peer_dense_kernel, tm=tm),
        grid=(m // tm, n_exp // te),
        in_specs=[pl.BlockSpec((tm, d), lambda i, e: (i, 0), pipeline_mode=pl.Buffered(1)),
                  pl.BlockSpec((te, d), lambda i, e: (e, 0)),
                  pl.BlockSpec((te, d), lambda i, e: (e, 0)),
                  pl.BlockSpec((tm, GATE_ROWS, N_KEYS), lambda i, e: (i, e // (GATE_ROWS * N_KEYS // te), 0)),
                  pl.BlockSpec(memory_space=pl.ANY)],
        out_specs=pl.BlockSpec(memory_space=pl.ANY),
        out_shape=jax.ShapeDtypeStruct((m, d), F32),
        scratch_shapes=[pltpu.VMEM((tm, d), F32)],
        compiler_params=pltpu.CompilerParams(dimension_semantics=("parallel", "arbitrary"),
                                             vmem_limit_bytes=PEER_DENSE_VMEM_LIMIT),
        name="peer_dense",
    )(xn, u_tab, v_tab, gates, h)


def _row_tile(m, cap):
    return min(m, cap)


def _trunk(x, attend, recur, w, lam_init):
    m, d = x.shape
    att_w = w["q_norm_cols"]
    xn = rmsnorm_bf16(x, w["norm1_w"], _row_tile(m, 256))
    z = matmul(xn, w["w_in"], tm=_row_tile(m, 1024), tn=512)
    qn, kn, knb, vb = qkv_post(z, w["q_norm_w"], w["k_norm_w"], att_w, _row_tile(m, 256))
    v = z[:, 2 * att_w:3 * att_w]
    att = attend(qn, kn, knb, v, vb)
    rnn, s_fin = recur(z)
    h = matmul((att, rnn), w["w_out"], x, tm=_row_tile(m, 1024), tn=512)
    hn = rmsnorm_bf16(h, w["norm2_w"], _row_tile(m, 256))
    pq = matmul(hn, w["peer_wq"], tm=_row_tile(m, 1024), tn=512)
    if "peer_tables_bf16" not in w:
        a, b, g, ub, vb16 = peer_topk(pq, w["peer_subkeys"], 128, tables=(w["peer_u"], w["peer_v"]))
        w["peer_tables_bf16"] = (ub, vb16)
    else:
        a, b, g = peer_topk(pq, w["peer_subkeys"], 128)
    gates = peer_gates(a.T, b.T, g.T, 64)
    y = peer_dense(hn, *w["peer_tables_bf16"], gates, h, _row_tile(m, 1024), 512)
    return y, kn, v, s_fin


def kernel(x_prompt, x_sample, cache_k, cache_v, state_rnn, page_table, norm1_w, w_in, q_norm_w, k_norm_w, lambda_q1, lambda_k1, lambda_q2, lambda_k2, sub_norm_w, lb_param, rnn_norm_w, w_out, norm2_w, peer_wq, peer_subkeys, peer_u, peer_v):
    depth = w_in.shape[0]
    assert depth == 1, "single-layer trunk"
    layer = 0
    batch, seq, d = x_prompt.shape
    n_dec, dec_seq, _ = x_sample.shape
    assert dec_seq == 1
    h_att = cache_k.shape[3]
    att_w = h_att * VAL_DIM
    h_rnn = state_rnn.shape[2]
    lam_init = 0.8 - 0.6 * math.exp(-0.3 * layer)
    rnn_col0 = 3 * att_w // LANES

    w = dict(norm1_w=norm1_w[layer], w_in=w_in[layer], q_norm_w=q_norm_w[layer], k_norm_w=k_norm_w[layer],
             w_out=w_out[layer], norm2_w=norm2_w[layer], peer_wq=peer_wq[layer],
             peer_subkeys=peer_subkeys[layer], peer_u=peer_u[layer], peer_v=peer_v[layer], q_norm_cols=att_w)
    lams = [p[layer].reshape(1, HEAD_DIM) for p in (lambda_q1, lambda_k1, lambda_q2, lambda_k2)]
    sub_w = sub_norm_w[layer]
    rnn_w = rnn_norm_w[layer]

    s0 = jnp.zeros((batch, h_rnn, RNN_DK, RNN_DV), F32)
    y_p, k_p, v_p, s_p = _trunk(
        x_prompt.reshape(batch * seq, d),
        lambda qn, kn, knb, v, vb: attn_prompt(qn, knb, vb, lams, sub_w, batch, seq, lam_init, ATTN_Q_BLOCK),
        lambda z: hgrn_prompt(z, lb_param, s0, rnn_w, batch, seq, rnn_col0),
        w, lam_init)

    m_pad = LANES
    xs = jnp.zeros((m_pad, d), F32).at[:n_dec].set(x_sample.reshape(n_dec, d))
    ck = cache_k[layer].reshape(cache_k.shape[1], PAGE_SIZE * h_att * 2, HEAD_DIM)
    cv = cache_v[layer]

    def attend_s(qn, kn, knb, v, vb):
        split = lambda a: a.reshape(-1, h_att, 2, HEAD_DIM)[:n_dec].astype(F32).transpose(0, 2, 1, 3)
        o = attn_sample(split(qn), split(kn), v[:n_dec].reshape(n_dec, h_att, VAL_DIM), ck, cv, page_table,
                        lams, sub_w, lam_init, SAMPLE_PAGES_PER_STEP)
        return jnp.zeros((m_pad, att_w), BF16).at[:n_dec].set(o.reshape(n_dec, att_w))

    def recur_s(z):
        o, s_new = hgrn_sample(z, lb_param, state_rnn[layer], rnn_w, rnn_col0)
        return jnp.zeros((m_pad, o.shape[1]), BF16).at[:n_dec].set(o), s_new

    y_s, k_s, v_s, s_s = _trunk(xs, attend_s, recur_s, w, lam_init)

    return (y_p.reshape(batch, seq, d),
            y_s[:n_dec].reshape(n_dec, 1, d),
            k_p.reshape(1, batch, seq, h_att, 2, HEAD_DIM),
            v_p.reshape(1, batch, seq, h_att, VAL_DIM),
            k_s.reshape(m_pad, h_att, 2, HEAD_DIM)[:n_dec].reshape(1, n_dec, 1, h_att, 2, HEAD_DIM),
            v_s[:n_dec].reshape(1, n_dec, 1, h_att, VAL_DIM),
            s_p[None].astype(state_rnn.dtype),
            s_s[None].astype(state_rnn.dtype))
```

```python
import functools
import math

import jax
import jax.numpy as jnp
from jax import lax
from jax.experimental import pallas as pl
from jax.experimental.pallas import tpu as pltpu

F32 = jnp.float32
BF16 = jnp.bfloat16

LANES = 128
LOG2_LANES = 7
HEAD_DIM = 128
VAL_DIM = 2 * HEAD_DIM
RNN_DK = 128
RNN_DV = 128
HGRN_CHUNK = 128
NORM_EPS = 1e-6
ATTN_SCALE = HEAD_DIM ** -0.5
LOG2_E = math.log2(math.e)
N_KEYS = 128
PEER_TOPK = 16
PAGE_SIZE = 128
GATE_ROWS = 8
SAMPLE_PAGES_PER_STEP = 8
ATTN_HEADS_PER_STEP = 4
ATTN_Q_BLOCK = 256
ATTN_KV_BLOCK = 1024
VMEM_LIMIT = 56 * 1024 * 1024
PEER_DENSE_VMEM_LIMIT = 60 * 1024 * 1024

NT_DIMS = (((1,), (1,)), ((), ()))
TN_DIMS = (((0,), (0,)), ((), ()))


def _params(*sem):
    return pltpu.CompilerParams(dimension_semantics=sem, vmem_limit_bytes=VMEM_LIMIT)


def _sigmoid(x):
    return 1.0 / (1.0 + jnp.exp(-x))


def _rmsnorm_kernel(x_ref, w_ref, o_ref):
    x = x_ref[...]
    ms = jnp.mean(x * x, axis=-1, keepdims=True)
    o_ref[...] = (x * lax.rsqrt(ms + NORM_EPS) * w_ref[...]).astype(o_ref.dtype)


def rmsnorm_bf16(x, w, tm):
    m, d = x.shape
    return pl.pallas_call(
        _rmsnorm_kernel,
        grid=(m // tm,),
        in_specs=[pl.BlockSpec((tm, d), lambda i: (i, 0)),
                  pl.BlockSpec((1, d), lambda i: (0, 0))],
        out_specs=pl.BlockSpec((tm, d), lambda i: (i, 0)),
        out_shape=jax.ShapeDtypeStruct((m, d), BF16),
        compiler_params=_params("parallel"),
        name="rmsnorm",
    )(x, w.reshape(1, d))


def _matmul_kernel(*refs, n_x, has_res):
    x_refs, w_ref = refs[:n_x], refs[n_x]
    r_ref = refs[n_x + 1] if has_res else None
    o_ref, wb_ref = refs[-2:]

    @pl.when(pl.program_id(1) == 0)
    def _():
        wb_ref[...] = w_ref[...].astype(BF16)

    acc, k0 = None, 0
    for x_ref in x_refs:
        part = jnp.dot(x_ref[...], wb_ref[k0:k0 + x_ref.shape[1], :], preferred_element_type=F32)
        acc = part if acc is None else acc + part
        k0 += x_ref.shape[1]
    if has_res:
        acc = r_ref[...] + acc
    o_ref[...] = acc


def matmul(xs, w, res=None, *, tm, tn):
    xs = xs if isinstance(xs, tuple) else (xs,)
    m = xs[0].shape[0]
    k, n = w.shape
    assert sum(x.shape[1] for x in xs) == k
    in_specs = [pl.BlockSpec((tm, x.shape[1]), lambda j, i: (i, 0)) for x in xs]
    in_specs.append(pl.BlockSpec((k, tn), lambda j, i: (0, j)))
    args = [*xs, w]
    if res is not None:
        in_specs.append(pl.BlockSpec((tm, tn), lambda j, i: (i, j)))
        args.append(res)
    return pl.pallas_call(
        functools.partial(_matmul_kernel, n_x=len(xs), has_res=res is not None),
        grid=(n // tn, m // tm),
        in_specs=in_specs,
        out_specs=pl.BlockSpec((tm, tn), lambda j, i: (i, j)),
        out_shape=jax.ShapeDtypeStruct((m, n), F32),
        scratch_shapes=[pltpu.VMEM((k, tn), BF16)],
        compiler_params=_params("arbitrary", "arbitrary"),
        name="matmul",
    )(*args)


def _qkv_kernel(zq_ref, zk_ref, zv_ref, qw_ref, kw_ref, qn_ref, kn_ref, knb_ref, vb_ref):
    tm, width = zq_ref.shape
    groups = width // HEAD_DIM
    for g in range(groups):
        sl = slice(g * HEAD_DIM, (g + 1) * HEAD_DIM)
        xq = zq_ref[:, sl]
        yq = xq * lax.rsqrt(jnp.mean(xq * xq, axis=-1, keepdims=True) + NORM_EPS) * qw_ref[...]
        qn_ref[:, sl] = yq.astype(BF16)
        xk = zk_ref[:, sl]
        yk = xk * lax.rsqrt(jnp.mean(xk * xk, axis=-1, keepdims=True) + NORM_EPS) * kw_ref[...]
        kn_ref[pl.ds(g, tm, stride=groups), :] = yk
        knb_ref[:, sl] = yk.astype(BF16)
    vb_ref[...] = zv_ref[...].astype(BF16)


def qkv_post(z, q_norm_w, k_norm_w, width, tm):
    m = z.shape[0]
    groups = width // HEAD_DIM
    blk = lambda c: pl.BlockSpec((tm, width), lambda i, c=c: (i, c))
    wspec = pl.BlockSpec((1, HEAD_DIM), lambda i: (0, 0))
    out = lambda dt: jax.ShapeDtypeStruct((m, width), dt)
    return pl.pallas_call(
        _qkv_kernel,
        grid=(m // tm,),
        in_specs=[blk(0), blk(1), blk(2), wspec, wspec],
        out_specs=[blk(0), pl.BlockSpec((tm * groups, HEAD_DIM), lambda i: (i, 0)), blk(0), blk(0)],
        out_shape=[out(BF16), jax.ShapeDtypeStruct((m * groups, HEAD_DIM), F32), out(BF16), out(BF16)],
        compiler_params=_params("parallel"),
        name="qkv_post",
    )(z, z, z, q_norm_w.reshape(1, HEAD_DIM), k_norm_w.reshape(1, HEAD_DIM))


def _lambda(lq1_ref, lk1_ref, lq2_ref, lk2_ref, lam_init):
    a = jnp.sum(lq1_ref[...] * lk1_ref[...], axis=-1, keepdims=True)
    b = jnp.sum(lq2_ref[...] * lk2_ref[...], axis=-1, keepdims=True)
    return jnp.exp(a) - jnp.exp(b) + lam_init


def _sub_norm(o, w, lam_init):
    ms = jnp.mean(o * o, axis=-1, keepdims=True)
    return o * lax.rsqrt(ms + NORM_EPS) * w * (1.0 - lam_init)


def _attn_prompt_kernel(q_ref, k_ref, v_ref, lq1_ref, lk1_ref, lq2_ref, lk2_ref, sw_ref, o_ref,
                        m_ref, l_ref, acc_ref, *, tq, lam_init):
    qi = pl.program_id(2)
    m_ref[...] = jnp.full(m_ref.shape, -jnp.inf, F32)
    l_ref[...] = jnp.zeros(l_ref.shape, F32)
    acc_ref[...] = jnp.zeros(acc_ref.shape, F32)
    tk = ATTN_KV_BLOCK
    n_full = (qi * tq + 1) // tk

    n_chain = m_ref.shape[0]

    def block(j, masked):
        kv_rows = pl.ds(pl.multiple_of(j * tk, tk), tk)
        for ch in range(n_chain):
            sl = slice(ch * HEAD_DIM, (ch + 1) * HEAD_DIM)
            vsl = slice(ch // 2 * VAL_DIM, (ch // 2 + 1) * VAL_DIM)
            s = lax.dot_general(q_ref[:, sl], k_ref[kv_rows, sl], NT_DIMS,
                                preferred_element_type=F32) * (ATTN_SCALE * LOG2_E)
            if masked:
                row = qi * tq + lax.broadcasted_iota(jnp.int32, (tq, tk), 0)
                col = j * tk + lax.broadcasted_iota(jnp.int32, (tq, tk), 1)
                s = jnp.where(col <= row, s, -jnp.inf)
            m_old = m_ref[ch]
            m_new = jnp.maximum(m_old, jnp.max(s, axis=-1, keepdims=True))
            alpha = jnp.exp2(m_old - m_new)
            p = jnp.exp2(s - m_new)
            p_lanes = sum(p[:, t * LANES:(t + 1) * LANES] for t in range(1, tk // LANES)) + p[:, :LANES]
            l_ref[ch] = alpha * l_ref[ch] + p_lanes
            acc_ref[ch] = alpha * acc_ref[ch] + jnp.dot(p.astype(BF16), v_ref[kv_rows, vsl],
                                                        preferred_element_type=F32)
            m_ref[ch] = m_new

    def full_block(j, carry):
        block(j, masked=False)
        return carry

    lax.fori_loop(0, n_full, full_block, 0)
    block(n_full, masked=True)
    lam = _lambda(lq1_ref, lk1_ref, lq2_ref, lk2_ref, lam_init)
    for hh in range(n_chain // 2):
        l0 = jnp.sum(l_ref[2 * hh], axis=-1, keepdims=True)
        l1 = jnp.sum(l_ref[2 * hh + 1], axis=-1, keepdims=True)
        o = acc_ref[2 * hh] / l0 - lam * (acc_ref[2 * hh + 1] / l1)
        o_ref[:, hh * VAL_DIM:(hh + 1) * VAL_DIM] = _sub_norm(o, sw_ref[...], lam_init).astype(o_ref.dtype)


def attn_prompt(qn, knb, vb, lams, sub_norm_w, batch, seq, lam_init, tq):
    m, width = qn.shape
    assert ATTN_KV_BLOCK % tq == 0 and seq % ATTN_KV_BLOCK == 0
    hp = ATTN_HEADS_PER_STEP
    blk_w = hp * VAL_DIM
    nq = seq // tq
    vec = pl.BlockSpec((1, HEAD_DIM), lambda b, h, i: (0, 0))
    return pl.pallas_call(
        functools.partial(_attn_prompt_kernel, tq=tq, lam_init=lam_init),
        grid=(batch, width // blk_w, nq),
        in_specs=[pl.BlockSpec((tq, blk_w), lambda b, h, i: (b * nq + i, h)),
                  pl.BlockSpec((seq, blk_w), lambda b, h, i: (b, h)),
                  pl.BlockSpec((seq, blk_w), lambda b, h, i: (b, h)),
                  vec, vec, vec, vec,
                  pl.BlockSpec((1, VAL_DIM), lambda b, h, i: (0, 0))],
        out_specs=pl.BlockSpec((tq, blk_w), lambda b, h, i: (b * nq + i, h)),
        out_shape=jax.ShapeDtypeStruct((m, width), BF16),
        scratch_shapes=[pltpu.VMEM((2 * hp, tq, 1), F32), pltpu.VMEM((2 * hp, tq, LANES), F32),
                        pltpu.VMEM((2 * hp, tq, VAL_DIM), F32)],
        compiler_params=_params("parallel", "parallel", "arbitrary"),
        name="attn_prompt",
    )(qn, knb, vb, *lams, sub_norm_w.reshape(1, VAL_DIM))


def _attn_sample_kernel(pt_ref, q_ref, kn_ref, vn_ref, lq1_ref, lk1_ref, lq2_ref, lk2_ref, sw_ref, *rest,
                        n_grp, lam_init):
    del pt_ref
    k_refs, v_refs = rest[:n_grp], rest[n_grp:2 * n_grp]
    o_ref, m_ref, l_ref, acc_ref = rest[2 * n_grp:]
    step = pl.program_id(1)
    heads = q_ref.shape[1]

    @pl.when(step == 0)
    def _():
        m_ref[...] = jnp.full(m_ref.shape, -jnp.inf, F32)
        l_ref[...] = jnp.zeros(l_ref.shape, F32)
        acc_ref[...] = jnp.zeros(acc_ref.shape, F32)

    q = [q_ref[c] * (ATTN_SCALE * LOG2_E) for c in range(2)]
    rows = PAGE_SIZE * heads
    r_blk = jnp.right_shift(lax.broadcasted_iota(jnp.int32, (VAL_DIM, VAL_DIM), 0), LOG2_LANES)
    c_blk = jnp.right_shift(lax.broadcasted_iota(jnp.int32, (VAL_DIM, VAL_DIM), 1), LOG2_LANES)
    ones_blk = jnp.where(r_blk == c_blk, 1.0, 0.0).astype(BF16)
    scores = []
    for g in range(n_grp):
        prod = [(k_refs[g][pl.ds(c, rows, stride=2), :].reshape(PAGE_SIZE, heads, HEAD_DIM) * q[c][None])
                .reshape(rows, HEAD_DIM) for c in range(2)]
        lhs = jnp.concatenate(prod, axis=-1).astype(BF16)
        scores.append(jnp.dot(lhs, ones_blk, preferred_element_type=F32).reshape(PAGE_SIZE, heads, VAL_DIM))
    for c in range(2):
        half = slice(c * HEAD_DIM, (c + 1) * HEAD_DIM)
        m_old = m_ref[c]
        m_new = m_old
        for s in scores:
            m_new = jnp.maximum(m_new, jnp.max(s[:, :, half], axis=0))
        alpha = jnp.exp2(m_old - m_new)
        l_new = alpha * l_ref[c]
        acc = jnp.concatenate([alpha, alpha], axis=-1) * acc_ref[c]
        for g in range(n_grp):
            p = jnp.exp2(scores[g][:, :, half] - m_new[None])
            l_new = l_new + jnp.sum(p, axis=0)
            acc = acc + jnp.sum(jnp.concatenate([p, p], axis=-1) * v_refs[g][...], axis=0)
        m_ref[c] = m_new
        l_ref[c] = l_new
        acc_ref[c] = acc

    @pl.when(step == pl.num_programs(1) - 1)
    def _():
        outs = []
        for c in range(2):
            s_new = jnp.sum(kn_ref[c] * q[c], axis=-1, keepdims=True)
            m_old = m_ref[c]
            m_new = jnp.maximum(m_old, s_new)
            alpha = jnp.exp2(m_old - m_new)
            p = jnp.exp2(s_new - m_new)
            wide = lambda a: jnp.concatenate([a, a], axis=-1)
            outs.append((wide(alpha) * acc_ref[c] + wide(p) * vn_ref[...]) / wide(alpha * l_ref[c] + p))
        lam = _lambda(lq1_ref, lk1_ref, lq2_ref, lk2_ref, lam_init)
        o_ref[...] = _sub_norm(outs[0] - lam * outs[1], sw_ref[...], lam_init).astype(o_ref.dtype)


def attn_sample(q, kn, vn, cache_k, cache_v, page_table, lams, sub_norm_w, lam_init, n_grp):
    bsz, _, heads, _ = q.shape
    n_pages = page_table.shape[1]
    qspec = pl.BlockSpec((None, 2, heads, HEAD_DIM), lambda b, s, pt: (b, 0, 0, 0))
    vspec = pl.BlockSpec((None, heads, VAL_DIM), lambda b, s, pt: (b, 0, 0))
    vec = pl.BlockSpec((1, HEAD_DIM), lambda b, s, pt: (0, 0))

    def page(shape, g):
        zeros = (0,) * (len(shape) - 1)
        return pl.BlockSpec((None,) + tuple(shape[1:]),
                            lambda b, s, pt: (pt[b * n_pages + s * n_grp + g],) + zeros)

    grid_spec = pltpu.PrefetchScalarGridSpec(
        num_scalar_prefetch=1,
        grid=(bsz, n_pages // n_grp),
        in_specs=[qspec, qspec, vspec, vec, vec, vec, vec, pl.BlockSpec((1, VAL_DIM), lambda b, s, pt: (0, 0))]
        + [page(cache_k.shape, g) for g in range(n_grp)] + [page(cache_v.shape, g) for g in range(n_grp)],
        out_specs=vspec,
        scratch_shapes=[pltpu.VMEM((2, heads, HEAD_DIM), F32), pltpu.VMEM((2, heads, HEAD_DIM), F32),
                        pltpu.VMEM((2, heads, VAL_DIM), F32)],
    )
    return pl.pallas_call(
        functools.partial(_attn_sample_kernel, n_grp=n_grp, lam_init=lam_init),
        grid_spec=grid_spec,
        out_shape=jax.ShapeDtypeStruct((bsz, heads, VAL_DIM), BF16),
        compiler_params=_params("parallel", "arbitrary"),
        name="attn_sample",
    )(page_table.reshape(-1), q, kn, vn, *lams, sub_norm_w.reshape(1, VAL_DIM),
      *([cache_k] * n_grp), *([cache_v] * n_grp))


def _lower_bound(lb_ref):
    p = lb_ref[...]
    e = jnp.exp(p - jnp.max(p, axis=0, keepdims=True))
    return e[0:1, :] / jnp.sum(e, axis=0, keepdims=True)


def _split3(x):
    hi = x.astype(BF16)
    r = x - hi.astype(F32)
    mid = r.astype(BF16)
    lo = (r - mid.astype(F32)).astype(BF16)
    return hi, mid, lo


def _hgrn_prompt_kernel(zq_ref, zf_ref, zi_ref, zg_ref, lb_ref, s0_ref, nw_ref, o_ref, sfin_ref,
                        q_s, k_s, b_s, f_s, oi_s, st_s, dec_s, *, seq):
    c = HGRN_CHUNK
    n_chunks = seq // c
    lb = _lower_bound(lb_ref)
    zf = zf_ref[...]
    sig = _sigmoid(zf)
    f = lb + (1.0 - lb) * sig
    f_s[...] = f
    k_s[...] = (1.0 - lb) * (1.0 - sig)
    zq = zq_ref[...]
    q_s[...] = zq * _sigmoid(zq)

    grp = 256
    r = lax.broadcasted_iota(jnp.int32, (grp, grp), 0)
    cc = lax.broadcasted_iota(jnp.int32, (grp, grp), 1)
    shift = c.bit_length() - 1
    same_chunk = jnp.right_shift(r, shift) == jnp.right_shift(cc, shift)
    tri = jnp.where(same_chunk & (cc <= r), 1.0, 0.0).astype(BF16)
    b_s[...] = jnp.log2(f)
    for gi in range(seq // grp):
        rows = slice(gi * grp, (gi + 1) * grp)
        hi, mid, lo = _split3(b_s[rows, :])
        b_s[rows, :] = (jnp.dot(tri, hi, preferred_element_type=F32)
                        + jnp.dot(tri, mid, preferred_element_type=F32)
                        + jnp.dot(tri, lo, preferred_element_type=F32))

    trow = lax.broadcasted_iota(jnp.int32, (c, RNN_DK), 0)
    arow = lax.broadcasted_iota(jnp.int32, (c, c), 0)
    acol = lax.broadcasted_iota(jnp.int32, (c, c), 1)
    sublane = 8

    def phase1(n, carry):
        base = pl.multiple_of(n * c, c)
        rows = pl.ds(base, c)
        b = b_s[rows, :]
        q = q_s[rows, :]
        k = k_s[rows, :]
        v = zi_ref[rows, :]
        b_last = b[c - 1:c, :]

        def b_row(r, nrows):
            return jnp.broadcast_to(b_s[pl.ds(base + r, 1), :], (nrows, RNN_DK))

        att = jnp.where(arow == acol, jnp.sum(q * k, axis=-1, keepdims=True), 0.0)
        m = 1
        while m < c:
            blk = 2 * m
            if m == 1:
                odd = (trow & 1) != 0
                qh = jnp.where(odd, q * f_s[rows, :], 0.0)
                kh = jnp.where(odd, 0.0, k)
            elif m < sublane:
                cands = [jnp.concatenate([b_row(t0 + i * blk + m - 1, sublane) for t0 in range(0, c, sublane)], axis=0)
                         for i in range(sublane // blk)]
                ref = cands[-1]
                for i in range(sublane // blk - 2, -1, -1):
                    ref = jnp.where((trow & (sublane - 1)) < (i + 1) * blk, cands[i], ref)
                e = b - ref
                x = jnp.exp2(jnp.minimum(e, -e))
                upper = (trow & m) != 0
                qh = jnp.where(upper, q * x, 0.0)
                kh = jnp.where(upper, 0.0, k * x)
            else:
                zeros = jnp.zeros((m, RNN_DK), F32)
                q_parts, k_parts = [], []
                for s0 in range(0, c, blk):
                    lo, up = slice(s0, s0 + m), slice(s0 + m, s0 + blk)
                    ref = b_row(s0 + m - 1, m)
                    q_parts += [zeros, q[up] * jnp.exp2(b[up] - ref)]
                    k_parts += [k[lo] * jnp.exp2(ref - b[lo]), zeros]
                qh = jnp.concatenate(q_parts, axis=0)
                kh = jnp.concatenate(k_parts, axis=0)
            a = lax.dot_general(qh.astype(BF16), kh.astype(BF16), NT_DIMS, preferred_element_type=F32)
            if blk < c:
                shift_blk = blk.bit_length() - 1
                a = jnp.where(jnp.right_shift(arow, shift_blk) == jnp.right_shift(acol, shift_blk), a, 0.0)
            att = att + a
            m = blk
        oi_s[rows, :] = jnp.dot(att.astype(BF16), v.astype(BF16), preferred_element_type=F32)
        q_s[rows, :] = q * jnp.exp2(b)
        kt = k * jnp.exp2(b_last - b)
        st_s[n] = lax.dot_general(v.astype(BF16), kt.astype(BF16), TN_DIMS, preferred_element_type=F32)
        dec_s[pl.ds(n, 1), :] = jnp.exp2(b_last)
        return carry

    lax.fori_loop(0, n_chunks, phase1, 0, unroll=4)

    def phase2(n, st):
        inc = st_s[n]
        st_s[n] = st
        return st * dec_s[pl.ds(n, 1), :] + inc

    st_fin = lax.fori_loop(0, n_chunks, phase2, s0_ref[...].T)
    sfin_ref[...] = st_fin.T

    def phase3(n, carry):
        rows = pl.ds(pl.multiple_of(n * c, c), c)
        oi_s[rows, :] += lax.dot_general(q_s[rows, :].astype(BF16), st_s[n].astype(BF16), NT_DIMS,
                                         preferred_element_type=F32)
        return carry

    lax.fori_loop(0, n_chunks, phase3, 0, unroll=4)
    o = oi_s[...]
    ms = jnp.mean(o * o, axis=-1, keepdims=True)
    zg = zg_ref[...]
    o_ref[...] = (o * lax.rsqrt(ms + NORM_EPS) * nw_ref[...] * (zg * _sigmoid(zg))).astype(o_ref.dtype)


def hgrn_prompt(z, lb_param, s0, rnn_norm_w, batch, seq, col0):
    m = z.shape[0]
    heads = s0.shape[1]
    zblk = lambda g: pl.BlockSpec((seq, RNN_DK), lambda b, h, g=g: (b, col0 + g * heads + h))
    n_chunks = seq // HGRN_CHUNK
    return pl.pallas_call(
        functools.partial(_hgrn_prompt_kernel, seq=seq),
        grid=(batch, heads),
        in_specs=[zblk(0), zblk(1), zblk(2), zblk(3),
                  pl.BlockSpec((lb_param.shape[0], RNN_DK), lambda b, h: (0, h)),
                  pl.BlockSpec((None, None, RNN_DK, RNN_DV), lambda b, h: (b, h, 0, 0)),
                  pl.BlockSpec((1, RNN_DV), lambda b, h: (0, 0))],
        out_specs=[pl.BlockSpec((seq, RNN_DV), lambda b, h: (b, h)),
                   pl.BlockSpec((None, None, RNN_DK, RNN_DV), lambda b, h: (b, h, 0, 0))],
        out_shape=[jax.ShapeDtypeStruct((m, heads * RNN_DV), BF16),
                   jax.ShapeDtypeStruct(s0.shape, F32)],
        scratch_shapes=[pltpu.VMEM((seq, RNN_DK), F32), pltpu.VMEM((seq, RNN_DK), F32),
                        pltpu.VMEM((seq, RNN_DK), F32), pltpu.VMEM((seq, RNN_DK), F32),
                        pltpu.VMEM((seq, RNN_DV), F32),
                        pltpu.VMEM((n_chunks, RNN_DV, RNN_DK), F32),
                        pltpu.VMEM((n_chunks, RNN_DK), F32)],
        compiler_params=_params("parallel", "parallel"),
        name="hgrn_prompt",
    )(z, z, z, z, lb_param, s0, rnn_norm_w.reshape(1, RNN_DV))


def _hgrn_sample_kernel(zq_ref, zf_ref, zi_ref, zg_ref, lb_ref, s0_ref, nw_ref, o_ref, s_ref):
    bsz = s0_ref.shape[0]
    lb = _lower_bound(lb_ref)
    zf = zf_ref[...]
    f_t = (lb + (1.0 - lb) * _sigmoid(zf)).T
    k_t = ((1.0 - lb) * _sigmoid(-zf)).T
    zq = zq_ref[...]
    q_t = (zq * _sigmoid(zq)).T
    for b in range(bsz):
        v = zi_ref[b:b + 1, :]
        s_new = f_t[:, b:b + 1] * s0_ref[b] + k_t[:, b:b + 1] * v
        s_ref[b] = s_new
        o = jnp.sum(q_t[:, b:b + 1] * s_new, axis=0, keepdims=True)
        ms = jnp.mean(o * o, axis=-1, keepdims=True)
        zg = zg_ref[b:b + 1, :]
        o_ref[b:b + 1, :] = (o * lax.rsqrt(ms + NORM_EPS) * nw_ref[...] * (zg * _sigmoid(zg))).astype(o_ref.dtype)


def hgrn_sample(z, lb_param, s0, rnn_norm_w, col0):
    bsz, heads = s0.shape[:2]
    zblk = lambda g: pl.BlockSpec((bsz, RNN_DK), lambda h, g=g: (0, col0 + g * heads + h))
    return pl.pallas_call(
        _hgrn_sample_kernel,
        grid=(heads,),
        in_specs=[zblk(0), zblk(1), zblk(2), zblk(3),
                  pl.BlockSpec((lb_param.shape[0], RNN_DK), lambda h: (0, h)),
                  pl.BlockSpec((bsz, None, RNN_DK, RNN_DV), lambda h: (0, h, 0, 0)),
                  pl.BlockSpec((1, RNN_DV), lambda h: (0, 0))],
        out_specs=[pl.BlockSpec((bsz, RNN_DV), lambda h: (0, h)),
                   pl.BlockSpec((bsz, None, RNN_DK, RNN_DV), lambda h: (0, h, 0, 0))],
        out_shape=[jax.ShapeDtypeStruct((bsz, heads * RNN_DV), BF16),
                   jax.ShapeDtypeStruct(s0.shape, F32)],
        compiler_params=_params("parallel"),
        name="hgrn_sample",
    )(z, z, z, z, lb_param, s0, rnn_norm_w.reshape(1, RNN_DV))


def _top16(s, ids):
    big = jnp.int32(2 ** 30)
    vals, sel = [], []
    for _ in range(PEER_TOPK):
        m = jnp.max(s, axis=0, keepdims=True)
        pick = jnp.min(jnp.where(s == m, ids, big), axis=0, keepdims=True)
        s = jnp.where(ids == pick, -jnp.inf, s)
        vals.append(m)
        sel.append(pick)
    return jnp.concatenate(vals, axis=0), jnp.concatenate(sel, axis=0)


def _peer_topk_kernel(*refs, cast_tables):
    if cast_tables:
        q_ref, sk_ref, u_ref, v_ref, a_ref, b_ref, g_ref, ub_ref, vb_ref = refs
        ub_ref[...] = u_ref[...].astype(BF16)
        vb_ref[...] = v_ref[...].astype(BF16)
    else:
        q_ref, sk_ref, a_ref, b_ref, g_ref = refs
    tm = q_ref.shape[0]
    heads = sk_ref.shape[0]
    key_ids = lax.broadcasted_iota(jnp.int32, (N_KEYS, tm), 0)
    k = PEER_TOPK
    for h in range(heads):
        half = []
        for c in range(2):
            col = (2 * h + c) * HEAD_DIM
            sc = lax.dot_general(sk_ref[h, c].astype(BF16), q_ref[:, col:col + HEAD_DIM].astype(BF16),
                                 NT_DIMS, preferred_element_type=F32)
            half.append(_top16(sc, key_ids))
        (s1, i1), (s2, i2) = half
        cand, eid, fid = [], [], []
        crow = lax.broadcasted_iota(jnp.int32, (k, tm), 0)
        for r in range(k // 2):
            nc = k if r == 0 else k // 2
            cand.append(s1[r:r + 1, :] + s2[:nc, :])
            eid.append(i1[r:r + 1, :] * N_KEYS + i2[:nc, :])
            fid.append(r * k + crow[:nc, :])
        cand.append(s1[k // 2:, :] + s2[0:1, :])
        eid.append(i1[k // 2:, :] * N_KEYS + i2[0:1, :])
        fid.append((k // 2 + crow[:k // 2, :]) * k)
        cand = jnp.concatenate(cand, axis=0)
        eid = jnp.concatenate(eid, axis=0)
        fid = jnp.concatenate(fid, axis=0)
        best, pick = _top16(cand, fid)
        e_sel = []
        for j in range(k):
            e_sel.append(jnp.max(jnp.where(fid == pick[j:j + 1, :], eid, -1), axis=0, keepdims=True))
        e_sel = jnp.concatenate(e_sel, axis=0)
        p = jnp.exp(best - best[0:1, :])
        gate = p / jnp.sum(p, axis=0, keepdims=True)
        rows = slice(h * k, (h + 1) * k)
        a_ref[rows, :] = jnp.right_shift(e_sel, LOG2_LANES)
        b_ref[rows, :] = jnp.bitwise_and(e_sel, N_KEYS - 1)
        g_ref[rows, :] = gate


def peer_topk(q, subkeys, tm, tables=None):
    m = q.shape[0]
    heads = subkeys.shape[0]
    n_sel = heads * PEER_TOPK
    steps = m // tm
    out = pl.BlockSpec((n_sel, tm), lambda i: (0, i))
    in_specs = [pl.BlockSpec((tm, q.shape[1]), lambda i: (i, 0)),
                pl.BlockSpec(subkeys.shape, lambda i: (0, 0, 0, 0))]
    out_specs = [out, out, out]
    out_shape = [jax.ShapeDtypeStruct((n_sel, m), jnp.int32),
                 jax.ShapeDtypeStruct((n_sel, m), jnp.int32),
                 jax.ShapeDtypeStruct((n_sel, m), F32)]
    args = [q, subkeys]
    if tables is not None:
        n_exp, d = tables[0].shape
        slab = pl.BlockSpec((n_exp // steps, d), lambda i: (i, 0))
        in_specs += [slab, slab]
        out_specs += [slab, slab]
        out_shape += [jax.ShapeDtypeStruct((n_exp, d), BF16)] * 2
        args += list(tables)
    return pl.pallas_call(
        functools.partial(_peer_topk_kernel, cast_tables=tables is not None),
        grid=(steps,),
        in_specs=in_specs,
        out_specs=out_specs,
        out_shape=out_shape,
        compiler_params=_params("parallel"),
        name="peer_topk",
    )(*args)


def _peer_gates_kernel(a_ref, b_ref, g_ref, o_ref):
    tg, n_sel = a_ref.shape
    ids = lax.broadcasted_iota(jnp.int32, (N_KEYS, n_sel), 0)

    def body(t, carry):
        row = pl.ds(t, 1)
        left = jnp.where(ids == a_ref[row, :], 1.0, 0.0).astype(BF16)
        right = jnp.where(ids == b_ref[row, :], g_ref[row, :], 0.0).astype(BF16)
        o_ref[t] = lax.dot_general(left, right, NT_DIMS, preferred_element_type=F32).astype(o_ref.dtype)
        return carry

    lax.fori_loop(0, tg, body, 0, unroll=64)


def peer_gates(a, b, g, tg):
    m, n_sel = a.shape
    spec = pl.BlockSpec((tg, n_sel), lambda i: (i, 0))
    return pl.pallas_call(
        _peer_gates_kernel,
        grid=(m // tg,),
        in_specs=[spec, spec, spec],
        out_specs=pl.BlockSpec((tg, N_KEYS, N_KEYS), lambda i: (i, 0, 0)),
        out_shape=jax.ShapeDtypeStruct((m, N_KEYS, N_KEYS), F32),
        compiler_params=_params("parallel"),
        name="peer_gates",
    )(a, b, g)


def _gelu(x):
    return 0.5 * x * (1.0 + lax.erf(x * (2.0 ** -0.5)))


def _peer_dense_kernel(x_ref, u_ref, v_ref, gs_ref, h_hbm, y_hbm, y_ref, *, tm):
    tile = pl.ds(pl.program_id(0) * tm, tm)

    @pl.when(pl.program_id(1) == 0)
    def _():
        pltpu.sync_copy(h_hbm.at[tile, :], y_ref)

    hid = lax.dot_general(x_ref[...], u_ref[...], NT_DIMS, preferred_element_type=F32)
    n_sub = hid.shape[1] // N_KEYS
    first = (pl.program_id(1) % (gs_ref.shape[1] // n_sub)) * n_sub
    gs2 = gs_ref.reshape(tm * GATE_ROWS, N_KEYS)
    w = []
    for j in range(n_sub):
        sl = slice(j * N_KEYS, (j + 1) * N_KEYS)
        gate = gs2[pl.ds(first + j, tm, stride=GATE_ROWS), :]
        w.append((gate * _gelu(hid[:, sl])).astype(BF16))
    w = jnp.concatenate(w, axis=-1)
    y_ref[...] += jnp.dot(w, v_ref[...], preferred_element_type=F32)

    @pl.when(pl.program_id(1) == pl.num_programs(1) - 1)
    def _():
        pltpu.sync_copy(y_ref, y_hbm.at[tile, :])


def peer_dense(xn, u_tab, v_tab, gates, h, tm, te):
    m, d = xn.shape
    n_exp = u_tab.shape[0]
    return pl.pallas_call(
        functools.partial(_peer_dense_kernel, tm=tm),
        grid=(m // tm, n_exp // te),
        in_specs=[pl.BlockSpec((tm, d), lambda i, e: (i, 0), pipeline_mode=pl.Buffered(1)),
                  pl.BlockSpec((te, d), lambda i, e: (e, 0)),
                  pl.BlockSpec((te, d), lambda i, e: (e, 0)),
                  pl.BlockSpec((tm, GATE_ROWS, N_KEYS), lambda i, e: (i, e // (GATE_ROWS * N_KEYS // te), 0)),
                  pl.BlockSpec(memory_space=pl.ANY)],
        out_specs=pl.BlockSpec(memory_space=pl.ANY),
        out_shape=jax.ShapeDtypeStruct((m, d), F32),
        scratch_shapes=[pltpu.VMEM((tm, d), F32)],
        compiler_params=pltpu.CompilerParams(dimension_semantics=("parallel", "arbitrary"),
                                             vmem_limit_bytes=PEER_DENSE_VMEM_LIMIT),
        name="peer_dense",
    )(xn, u_tab, v_tab, gates, h)


ELEMENTWISE_ROWS = 256
MATMUL_ROWS = 1024
MATMUL_COLS = 512
TOPK_TOKENS = 128
GATE_TOKENS = 64
PEER_ROWS = 1024
PEER_EXPERTS = 512


def _row_tile(m, cap):
    return min(m, cap)


def _trunk(x, attend, recur, w, lam_init):
    m, d = x.shape
    att_w = w["q_norm_cols"]
    ew_rows, mm_rows = _row_tile(m, ELEMENTWISE_ROWS), _row_tile(m, MATMUL_ROWS)
    xn = rmsnorm_bf16(x, w["norm1_w"], ew_rows)
    z = matmul(xn, w["w_in"], tm=mm_rows, tn=MATMUL_COLS)
    qn, kn, knb, vb = qkv_post(z, w["q_norm_w"], w["k_norm_w"], att_w, ew_rows)
    v = z[:, 2 * att_w:3 * att_w]
    att = attend(qn, kn, knb, v, vb)
    rnn, s_fin = recur(z)
    h = matmul((att, rnn), w["w_out"], x, tm=mm_rows, tn=MATMUL_COLS)
    hn = rmsnorm_bf16(h, w["norm2_w"], ew_rows)
    pq = matmul(hn, w["peer_wq"], tm=mm_rows, tn=MATMUL_COLS)
    if "peer_tables_bf16" not in w:
        a, b, g, ub, vb16 = peer_topk(pq, w["peer_subkeys"], TOPK_TOKENS, tables=(w["peer_u"], w["peer_v"]))
        w["peer_tables_bf16"] = (ub, vb16)
    else:
        a, b, g = peer_topk(pq, w["peer_subkeys"], TOPK_TOKENS)
    gates = peer_gates(a.T, b.T, g.T, GATE_TOKENS)
    y = peer_dense(hn, *w["peer_tables_bf16"], gates, h, _row_tile(m, PEER_ROWS), PEER_EXPERTS)
    return y, kn, v, s_fin


def kernel(x_prompt, x_sample, cache_k, cache_v, state_rnn, page_table, norm1_w, w_in, q_norm_w, k_norm_w, lambda_q1, lambda_k1, lambda_q2, lambda_k2, sub_norm_w, lb_param, rnn_norm_w, w_out, norm2_w, peer_wq, peer_subkeys, peer_u, peer_v):
    depth = w_in.shape[0]
    assert depth == 1, "single-layer trunk"
    layer = 0
    batch, seq, d = x_prompt.shape
    n_dec, dec_seq, _ = x_sample.shape
    assert dec_seq == 1
    h_att = cache_k.shape[3]
    att_w = h_att * VAL_DIM
    h_rnn = state_rnn.shape[2]
    lam_init = 0.8 - 0.6 * math.exp(-0.3 * layer)
    rnn_col0 = 3 * att_w // LANES

    w = dict(norm1_w=norm1_w[layer], w_in=w_in[layer], q_norm_w=q_norm_w[layer], k_norm_w=k_norm_w[layer],
             w_out=w_out[layer], norm2_w=norm2_w[layer], peer_wq=peer_wq[layer],
             peer_subkeys=peer_subkeys[layer], peer_u=peer_u[layer], peer_v=peer_v[layer], q_norm_cols=att_w)
    lams = [p[layer].reshape(1, HEAD_DIM) for p in (lambda_q1, lambda_k1, lambda_q2, lambda_k2)]
    sub_w = sub_norm_w[layer]
    rnn_w = rnn_norm_w[layer]

    s0 = jnp.zeros((batch, h_rnn, RNN_DK, RNN_DV), F32)
    y_p, k_p, v_p, s_p = _trunk(
        x_prompt.reshape(batch * seq, d),
        lambda qn, kn, knb, v, vb: attn_prompt(qn, knb, vb, lams, sub_w, batch, seq, lam_init, ATTN_Q_BLOCK),
        lambda z: hgrn_prompt(z, lb_param, s0, rnn_w, batch, seq, rnn_col0),
        w, lam_init)

    m_pad = LANES
    xs = jnp.zeros((m_pad, d), F32).at[:n_dec].set(x_sample.reshape(n_dec, d))
    ck = cache_k[layer].reshape(cache_k.shape[1], PAGE_SIZE * h_att * 2, HEAD_DIM)
    cv = cache_v[layer]

    def attend_s(qn, kn, knb, v, vb):
        split = lambda a: a.reshape(-1, h_att, 2, HEAD_DIM)[:n_dec].astype(F32).transpose(0, 2, 1, 3)
        o = attn_sample(split(qn), split(kn), v[:n_dec].reshape(n_dec, h_att, VAL_DIM), ck, cv, page_table,
                        lams, sub_w, lam_init, SAMPLE_PAGES_PER_STEP)
        return jnp.zeros((m_pad, att_w), BF16).at[:n_dec].set(o.reshape(n_dec, att_w))

    def recur_s(z):
        o, s_new = hgrn_sample(z, lb_param, state_rnn[layer], rnn_w, rnn_col0)
        return jnp.zeros((m_pad, o.shape[1]), BF16).at[:n_dec].set(o), s_new

    y_s, k_s, v_s, s_s = _trunk(xs, attend_s, recur_s, w, lam_init)

    return (y_p.reshape(batch, seq, d),
            y_s[:n_dec].reshape(n_dec, 1, d),
            k_p.reshape(1, batch, seq, h_att, 2, HEAD_DIM),
            v_p.reshape(1, batch, seq, h_att, VAL_DIM),
            k_s.reshape(m_pad, h_att, 2, HEAD_DIM)[:n_dec].reshape(1, n_dec, 1, h_att, 2, HEAD_DIM),
            v_s[:n_dec].reshape(1, n_dec, 1, h_att, VAL_DIM),
            s_p[None].astype(state_rnn.dtype),
            s_s[None].astype(state_rnn.dtype))
```
